```python
import math
import jax, jax.numpy as jnp
from jax import lax
import numpy as np

D_MODEL = 1024
BATCH = 2
SEQ = 16384
DEPTH = 4

SSM_HEADS = 16
SSM_HEAD_DIM = 64
SSM_WIDTH = SSM_HEADS * SSM_HEAD_DIM
SSM_GROUPS = 4
SSM_STATE = 128
SSM_CHUNK = 128
SSM_CONV_DIM = SSM_WIDTH + 2 * SSM_GROUPS * SSM_STATE
GDN_HEADS = 8
GDN_HEAD_DIM = 128
GDN_WIDTH = GDN_HEADS * GDN_HEAD_DIM
GDN_CHUNK = 64
GDN_CONV_DIM = 3 * GDN_WIDTH
CONV_WIDTH = 4
MIX_WIDTH = SSM_WIDTH + GDN_WIDTH
D_FF = 4 * D_MODEL
EPS = 1e-6
OFF_Z = SSM_WIDTH
OFF_XBC = OFF_Z + SSM_CONV_DIM
OFF_DT = OFF_XBC + SSM_HEADS
OFF_QKV = OFF_DT + GDN_CONV_DIM
OFF_GATE = OFF_QKV + GDN_WIDTH
OFF_BETA = OFF_GATE + GDN_HEADS
IN_PROJ = OFF_BETA + GDN_HEADS

kernel_name = "hymba_style_ssd_gdn_hybrid"


def rmsnorm(x, w):
    xf = x.astype(jnp.float32)
    y = xf * lax.rsqrt(jnp.mean(xf * xf, axis=-1, keepdims=True) + EPS)
    return (y * w.astype(jnp.float32)).astype(x.dtype)


def rmsnorm_f32(x, w):
    y = x * lax.rsqrt(jnp.mean(x * x, axis=-1, keepdims=True) + EPS)
    return y * w.astype(jnp.float32)


def l2norm(x):
    return x * lax.rsqrt(jnp.sum(x * x, axis=-1, keepdims=True) + EPS)


def causal_conv(x, w):
    k = w.shape[0]
    return lax.conv_general_dilated(
        x, w[:, None, :], window_strides=(1,), padding=((k - 1, 0),),
        dimension_numbers=("NWC", "WIO", "NWC"), feature_group_count=x.shape[-1])


def ssd_chunked(x, dt, a_neg, b_in, c_in):
    bsz, seqlen = x.shape[0], x.shape[1]
    nc = seqlen // SSM_CHUNK
    e = SSM_HEADS // SSM_GROUPS
    xs = (x * dt[..., None]).reshape(bsz, nc, SSM_CHUNK, SSM_GROUPS, e, SSM_HEAD_DIM)
    a = jnp.moveaxis((dt * a_neg).reshape(bsz, nc, SSM_CHUNK, SSM_GROUPS, e), 2, -1)
    a_cs = jnp.cumsum(a, axis=-1)
    bc = b_in.reshape(bsz, nc, SSM_CHUNK, SSM_GROUPS, SSM_STATE)
    cc = c_in.reshape(bsz, nc, SSM_CHUNK, SSM_GROUPS, SSM_STATE)
    causal = jnp.tril(jnp.ones((SSM_CHUNK, SSM_CHUNK), dtype=bool))
    decay_in = jnp.exp(jnp.where(causal, a_cs[..., :, None] - a_cs[..., None, :], -jnp.inf))
    cb = jnp.einsum('bclgn,bcsgn->bcgls', cc, bc)
    y_diag = jnp.einsum('bcgls,bcgels,bcsgep->bclgep', cb, decay_in, xs)
    decay_to_end = jnp.exp(a_cs[..., -1:] - a_cs)
    states = jnp.einsum('bclgn,bcgel,bclgep->bcgepn', bc, decay_to_end, xs)
    chunk_decay = jnp.exp(a_cs[..., -1])

    def step(h, inp):
        dec, st = inp
        return h * dec[..., None, None] + st, h

    h0 = jnp.zeros((bsz, SSM_GROUPS, e, SSM_HEAD_DIM, SSM_STATE), jnp.float32)
    _, prev = lax.scan(step, h0, (jnp.moveaxis(chunk_decay, 1, 0), jnp.moveaxis(states, 1, 0)))
    prev = jnp.moveaxis(prev, 0, 1)
    y_off = jnp.einsum('bclgn,bcgepn,bcgel->bclgep', cc, prev, jnp.exp(a_cs))
    return (y_diag + y_off).reshape(bsz, seqlen, SSM_HEADS, SSM_HEAD_DIM)


def gated_delta_rule_chunked(q, k, v, g, beta):
    bsz, seqlen = q.shape[0], q.shape[1]
    nc = seqlen // GDN_CHUNK

    def to_chunks(t):
        t = jnp.moveaxis(t, 2, 1)
        return t.reshape((bsz, GDN_HEADS, nc, GDN_CHUNK) + t.shape[3:])

    q = to_chunks(q) * (GDN_HEAD_DIM ** -0.5)
    k, v, g, beta = to_chunks(k), to_chunks(v), to_chunks(g), to_chunks(beta)
    g_cs = jnp.cumsum(g, axis=-1)
    incl = jnp.tril(jnp.ones((GDN_CHUNK, GDN_CHUNK), dtype=bool))
    strict = jnp.tril(jnp.ones((GDN_CHUNK, GDN_CHUNK), dtype=bool), -1)
    decay = jnp.exp(jnp.where(incl, g_cs[..., :, None] - g_cs[..., None, :], -jnp.inf))
    k_beta = k * beta[..., None]
    a_strict = jnp.where(strict, jnp.einsum('bhnid,bhnjd->bhnij', k_beta, k) * decay, 0.0)
    t_mat = a_strict + jnp.eye(GDN_CHUNK, dtype=jnp.float32)
    rhs = jnp.concatenate([v * beta[..., None], k_beta * jnp.exp(g_cs)[..., None]], axis=-1)
    sol = lax.linalg.triangular_solve(t_mat, rhs, left_side=True, lower=True, unit_diagonal=True)
    u, w = sol[..., :GDN_HEAD_DIM], sol[..., GDN_HEAD_DIM:]
    qk = jnp.where(incl, jnp.einsum('bhnid,bhnjd->bhnij', q, k) * decay, 0.0)
    q_dec = q * jnp.exp(g_cs)[..., None]
    g_last = g_cs[..., -1]
    k_dec = k * jnp.exp(g_last[..., None] - g_cs)[..., None]

    def step(s, inp):
        qk_c, qd_c, kd_c, u_c, w_c, gl_c = inp
        v_new = u_c - jnp.einsum('bhck,bhkv->bhcv', w_c, s)
        o = jnp.einsum('bhck,bhkv->bhcv', qd_c, s) + jnp.einsum('bhij,bhjv->bhiv', qk_c, v_new)
        s = s * jnp.exp(gl_c)[..., None, None] + jnp.einsum('bhck,bhcv->bhkv', kd_c, v_new)
        return s, o

    s0 = jnp.zeros((bsz, GDN_HEADS, GDN_HEAD_DIM, GDN_HEAD_DIM), jnp.float32)
    xs = tuple(jnp.moveaxis(t, 2, 0) for t in (qk, q_dec, k_dec, u, w, g_last))
    _, o = lax.scan(step, s0, xs)
    o = jnp.moveaxis(o, 0, 2).reshape(bsz, GDN_HEADS, seqlen, GDN_HEAD_DIM)
    return jnp.moveaxis(o, 1, 2)


def hybrid_mixer(h, w_in, ssm_conv_w, ssm_conv_b, ssm_dt_bias, ssm_a_log, ssm_d, ssm_norm_w,
                 gdn_conv_w, gdn_dt_bias, gdn_a_log, gdn_norm_w, w_out):
    bsz, seqlen, _ = h.shape
    f32 = jnp.float32
    proj = h @ w_in
    z, xbc, dt_raw, qkv, gate, beta_raw, a_raw = jnp.split(
        proj, [OFF_Z, OFF_XBC, OFF_DT, OFF_QKV, OFF_GATE, OFF_BETA], axis=-1)
    xbc = jax.nn.silu(causal_conv(xbc, ssm_conv_w) + ssm_conv_b).astype(f32)
    xs, b_in, c_in = jnp.split(xbc, [SSM_WIDTH, SSM_WIDTH + SSM_GROUPS * SSM_STATE], axis=-1)
    dt = jax.nn.softplus(dt_raw.astype(f32) + ssm_dt_bias.astype(f32))
    a_neg = -jnp.exp(ssm_a_log.astype(f32))
    xh = xs.reshape(bsz, seqlen, SSM_HEADS, SSM_HEAD_DIM)
    y = ssd_chunked(xh, dt, a_neg,
                    b_in.reshape(bsz, seqlen, SSM_GROUPS, SSM_STATE),
                    c_in.reshape(bsz, seqlen, SSM_GROUPS, SSM_STATE))
    y = y + ssm_d.astype(f32)[:, None] * xh
    y = y.reshape(bsz, seqlen, SSM_WIDTH) * jax.nn.silu(z.astype(f32))
    y = rmsnorm_f32(y.reshape(bsz, seqlen, SSM_GROUPS, SSM_WIDTH // SSM_GROUPS),
                    ssm_norm_w.reshape(SSM_GROUPS, SSM_WIDTH // SSM_GROUPS))
    y = y.reshape(bsz, seqlen, SSM_WIDTH)
    qkv = jax.nn.silu(causal_conv(qkv, gdn_conv_w)).astype(f32)
    q, k, v = jnp.split(qkv, [GDN_WIDTH, 2 * GDN_WIDTH], axis=-1)
    q = l2norm(q.reshape(bsz, seqlen, GDN_HEADS, GDN_HEAD_DIM))
    k = l2norm(k.reshape(bsz, seqlen, GDN_HEADS, GDN_HEAD_DIM))
    v = v.reshape(bsz, seqlen, GDN_HEADS, GDN_HEAD_DIM)
    beta = jax.nn.sigmoid(beta_raw.astype(f32))
    g = -jnp.exp(gdn_a_log.astype(f32)) * jax.nn.softplus(a_raw.astype(f32) + gdn_dt_bias.astype(f32))
    o = gated_delta_rule_chunked(q, k, v, g, beta)
    o = rmsnorm_f32(o, gdn_norm_w) * jax.nn.silu(
        gate.astype(f32).reshape(bsz, seqlen, GDN_HEADS, GDN_HEAD_DIM))
    o = o.reshape(bsz, seqlen, GDN_WIDTH)
    mixed = jnp.concatenate([y, o], axis=-1).astype(h.dtype)
    return mixed @ w_out


def setup_inputs(seed: int = 0) -> dict:
    key = jax.random.key(seed)
    ks = jax.random.split(key, 20)
    f32 = jnp.float32

    def nrm(k, shape, scale):
        return jax.random.normal(k, shape, f32) * scale

    def gain(k, shape):
        return 1.0 + 0.02 * jax.random.normal(k, shape, f32)

    def dt_bias(k, shape):
        dt = jnp.exp(jax.random.uniform(k, shape, f32, math.log(1e-3), math.log(1e-1)))
        return dt + jnp.log(-jnp.expm1(-dt))

    return {
        "x": jax.random.normal(ks[0], (BATCH, SEQ, D_MODEL), f32),
        "attn_norm_w": gain(ks[1], (DEPTH, D_MODEL)),
        "w_in": nrm(ks[2], (DEPTH, D_MODEL, IN_PROJ), D_MODEL ** -0.5),
        "ssm_conv_w": nrm(ks[3], (DEPTH, CONV_WIDTH, SSM_CONV_DIM), CONV_WIDTH ** -0.5),
        "ssm_conv_b": nrm(ks[4], (DEPTH, SSM_CONV_DIM), 0.02),
        "ssm_dt_bias": dt_bias(ks[5], (DEPTH, SSM_HEADS)),
        "ssm_a_log": jnp.log(jax.random.uniform(ks[6], (DEPTH, SSM_HEADS), f32, 1.0, 16.0)),
        "ssm_d": gain(ks[7], (DEPTH, SSM_HEADS)),
        "ssm_norm_w": gain(ks[8], (DEPTH, SSM_WIDTH)),
        "gdn_conv_w": nrm(ks[9], (DEPTH, CONV_WIDTH, GDN_CONV_DIM), CONV_WIDTH ** -0.5),
        "gdn_dt_bias": dt_bias(ks[10], (DEPTH, GDN_HEADS)),
        "gdn_a_log": jnp.log(jax.random.uniform(ks[11], (DEPTH, GDN_HEADS), f32, 1.0, 16.0)),
        "gdn_norm_w": gain(ks[12], (DEPTH, GDN_HEAD_DIM)),
        "w_out": nrm(ks[13], (DEPTH, MIX_WIDTH, D_MODEL), MIX_WIDTH ** -0.5),
        "mlp_norm_w": gain(ks[14], (DEPTH, D_MODEL)),
        "w_up": nrm(ks[15], (DEPTH, D_MODEL, D_FF), D_MODEL ** -0.5),
        "w_down": nrm(ks[16], (DEPTH, D_FF, D_MODEL), D_FF ** -0.5),
        "final_norm_w": gain(ks[17], (D_MODEL,)),
    }


def reference(x, attn_norm_w, w_in, ssm_conv_w, ssm_conv_b, ssm_dt_bias, ssm_a_log, ssm_d,
              ssm_norm_w, gdn_conv_w, gdn_dt_bias, gdn_a_log, gdn_norm_w, w_out,
              mlp_norm_w, w_up, w_down, final_norm_w):
    h = x
    for layer in range(DEPTH):
        hn = rmsnorm(h, attn_norm_w[layer])
        h = h + hybrid_mixer(hn, w_in[layer], ssm_conv_w[layer], ssm_conv_b[layer],
                             ssm_dt_bias[layer], ssm_a_log[layer], ssm_d[layer], ssm_norm_w[layer],
                             gdn_conv_w[layer], gdn_dt_bias[layer], gdn_a_log[layer],
                             gdn_norm_w[layer], w_out[layer])
        hn = rmsnorm(h, mlp_norm_w[layer])
        h = h + jnp.square(jax.nn.relu(hn @ w_up[layer])) @ w_down[layer]
    return rmsnorm(h, final_norm_w)
```

```python
import functools

import jax
import jax.numpy as jnp
from jax import lax
from jax.experimental import pallas as pl
from jax.experimental.pallas import tpu as pltpu

F32 = jnp.float32
BF16 = jnp.bfloat16

D_MODEL = 1024
SSM_HEADS = 16
SSM_HEAD_DIM = 64
SSM_WIDTH = SSM_HEADS * SSM_HEAD_DIM
SSM_GROUPS = 4
SSM_STATE = 128
SSM_GROUP_WIDTH = SSM_WIDTH // SSM_GROUPS
GDN_HEADS = 8
GDN_HEAD_DIM = 128
GDN_WIDTH = GDN_HEADS * GDN_HEAD_DIM
CONV_WIDTH = 4
MIX_WIDTH = SSM_WIDTH + GDN_WIDTH
D_FF = 4 * D_MODEL
EPS = 1e-6

OFF_Z = 0
OFF_X = SSM_WIDTH
OFF_B = OFF_X + SSM_WIDTH
OFF_C = OFF_B + SSM_GROUPS * SSM_STATE
OFF_DT = OFF_C + SSM_GROUPS * SSM_STATE
OFF_Q = OFF_DT + SSM_HEADS
OFF_K = OFF_Q + GDN_WIDTH
OFF_V = OFF_K + GDN_WIDTH
OFF_GATE = OFF_V + GDN_WIDTH
OFF_BETA = OFF_GATE + GDN_WIDTH
OFF_A = OFF_BETA + GDN_HEADS
IN_PROJ = OFF_A + GDN_HEADS

LANES = 128
SUBLANES = 8
CHUNK = 128
COL_DT = 0
COL_BETA = SSM_HEADS
COL_G = SSM_HEADS + GDN_HEADS
SPLIT_LANES = 32
STREAMS = (("z", SSM_WIDTH), ("x", SSM_WIDTH), ("b", SSM_GROUPS * SSM_STATE), ("c", SSM_GROUPS * SSM_STATE),
           ("q", GDN_WIDTH), ("k", GDN_WIDTH), ("v", GDN_WIDTH), ("gate", GDN_WIDTH), ("small", LANES))
IN_PROJ_PAD = sum(w for _, w in STREAMS)
CONV_STREAMS = (("x", SSM_WIDTH), ("b", 512), ("c", 512), ("q", GDN_WIDTH), ("k", GDN_WIDTH), ("v", GDN_WIDTH))
CONV_TOTAL = sum(w for _, w in CONV_STREAMS)
NEG_BIG = -1e30
VMEM_LIMIT_BYTES = 56 * 1024 * 1024

TM_PROJ = 512
T_MIX = 256


def _silu(x):
    return x / (1.0 + jnp.exp(-x))


def _softplus(x):
    return jnp.maximum(x, 0.0) + jnp.log1p(jnp.exp(-jnp.abs(x)))


def _dot(a, b):
    return jnp.dot(a, b, preferred_element_type=F32)


def _dot_nt(a, b):
    return lax.dot_general(a, b, (((1,), (1,)), ((), ())), preferred_element_type=F32)


def _compress3(y):
    hi = y.astype(BF16).astype(F32)
    r1 = y - hi
    mid = r1.astype(BF16).astype(F32)
    lo = r1 - mid
    return (hi + pltpu.roll(mid, SPLIT_LANES, 1) + pltpu.roll(lo, 2 * SPLIT_LANES, 1)).astype(BF16)


def _lane_iota(shape):
    return lax.broadcasted_iota(jnp.int32, shape, len(shape) - 1)


def _in_proj_body(h_ref, nw_ref, w_ref, *out_refs):
    h = h_ref[...]
    ms = jnp.mean(h * h, axis=-1, keepdims=True)
    hn = (h * lax.rsqrt(ms + EPS) * nw_ref[...]).astype(BF16)
    off = 0
    for ref, (_, width) in zip(out_refs, STREAMS):
        ref[...] = _dot(hn, w_ref[:, off:off + width]).astype(ref.dtype)
        off += width


def _in_proj(h2d, norm_w, w_r):
    n = h2d.shape[0]
    tm = min(TM_PROJ, n)
    grid = (n // tm,)
    out_shape = tuple(jax.ShapeDtypeStruct((n, w), F32 if name == "small" else BF16) for name, w in STREAMS)
    out_specs = tuple(pl.BlockSpec((tm, w), lambda i: (i, 0)) for _, w in STREAMS)
    return pl.pallas_call(
        _in_proj_body,
        grid=grid,
        in_specs=[
            pl.BlockSpec((tm, D_MODEL), lambda i: (i, 0)),
            pl.BlockSpec((1, D_MODEL), lambda i: (0, 0)),
            pl.BlockSpec((D_MODEL, IN_PROJ_PAD), lambda i: (0, 0), pipeline_mode=pl.Buffered(1)),
        ],
        out_specs=out_specs,
        out_shape=out_shape,
        compiler_params=pltpu.CompilerParams(dimension_semantics=("arbitrary",),
                                             vmem_limit_bytes=VMEM_LIMIT_BYTES),
        name="in_proj",
    )(h2d, norm_w, w_r)


def _out_mlp_body(mixed_ref, h_ref, wout_ref, nw_ref, wup_ref, wdown_ref, fnw_ref, o_ref, *, final):
    h1 = h_ref[...] + _dot(mixed_ref[...], wout_ref[...])
    ms = jnp.mean(h1 * h1, axis=-1, keepdims=True)
    hn = (h1 * lax.rsqrt(ms + EPS) * nw_ref[...]).astype(BF16)
    acc = h1
    for j in range(D_FF // D_MODEL):
        cols = slice(j * D_MODEL, (j + 1) * D_MODEL)
        up = _dot(hn, wup_ref[:, cols])
        act = jnp.square(jnp.maximum(up, 0.0)).astype(BF16)
        acc = acc + _dot(act, wdown_ref[cols, :])
    if final:
        ms2 = jnp.mean(acc * acc, axis=-1, keepdims=True)
        acc = acc * lax.rsqrt(ms2 + EPS) * fnw_ref[...]
    o_ref[...] = acc


def _out_mlp(mixed, h2d, w_out, norm_w, w_up, w_down, final_w, final):
    n = h2d.shape[0]
    tm = min(TM_PROJ, n)
    const = dict(pipeline_mode=pl.Buffered(1))
    return pl.pallas_call(
        functools.partial(_out_mlp_body, final=final),
        grid=(n // tm,),
        in_specs=[
            pl.BlockSpec((tm, MIX_WIDTH), lambda i: (i, 0)),
            pl.BlockSpec((tm, D_MODEL), lambda i: (i, 0)),
            pl.BlockSpec((MIX_WIDTH, D_MODEL), lambda i: (0, 0), **const),
            pl.BlockSpec((1, D_MODEL), lambda i: (0, 0)),
            pl.BlockSpec((D_MODEL, D_FF), lambda i: (0, 0), **const),
            pl.BlockSpec((D_FF, D_MODEL), lambda i: (0, 0), **const),
            pl.BlockSpec((1, D_MODEL), lambda i: (0, 0)),
        ],
        out_specs=pl.BlockSpec((tm, D_MODEL), lambda i: (i, 0)),
        out_shape=jax.ShapeDtypeStruct((n, D_MODEL), F32),
        compiler_params=pltpu.CompilerParams(dimension_semantics=("arbitrary",),
                                             vmem_limit_bytes=VMEM_LIMIT_BYTES),
        name="out_mlp",
    )(mixed, h2d, w_out, norm_w, w_up, w_down, final_w)


EXP_DT = 0
EXP_W = EXP_DT + SSM_WIDTH
EXP_EA = EXP_W + SSM_WIDTH
EXP_ACOL = EXP_EA + SSM_WIDTH
EXP_BETA = EXP_ACOL + SSM_HEADS * LANES
EXP_GCS = EXP_BETA + GDN_WIDTH
EXP_TOTAL = EXP_GCS + GDN_WIDTH


def _mixer_body(z_ref, x_ref, b_ref, c_ref, q_ref, k_ref, v_ref, gate_ref, small_ref,
                sconv_w_ref, sconv_b_ref, gconv_w_ref, prow_ref, drow_ref, snorm_ref, gnorm_ref,
                tri_ref, e_ssd_ref, e_col_ref, e_beta_ref, e_g_ref,
                out_ref,
                pad_ref, act_ref, exp_ref, xdt_ref, xw_ref, y_ref, o_ref, ht_ref, s_ref, *, tile):
    nchunk = tile // CHUNK
    t_idx = pl.program_id(1)

    @pl.when(t_idx == 0)
    def _():
        pad_ref[0:SUBLANES, :] = jnp.zeros((SUBLANES, CONV_TOTAL), F32)
        ht_ref[...] = jnp.zeros_like(ht_ref)
        s_ref[...] = jnp.zeros_like(s_ref)

    raw_refs = dict(x=x_ref, b=b_ref, c=c_ref, q=q_ref, k=k_ref, v=v_ref)
    off = 0
    for name, width in CONV_STREAMS:
        is_ssm = name in ("x", "b", "c")
        w_ref = sconv_w_ref if is_ssm else gconv_w_ref
        w_off = off if is_ssm else off - 2 * SSM_WIDTH
        step = LANES if name in ("q", "k") else 512
        for c0 in range(0, width, step):
            cols = slice(off + c0, off + c0 + step)
            wcols = slice(w_off + c0, w_off + c0 + step)
            pad_ref[SUBLANES:SUBLANES + tile, cols] = raw_refs[name][:, c0:c0 + step].astype(F32)
            acc = None
            for j in range(CONV_WIDTH):
                start = SUBLANES - (CONV_WIDTH - 1) + j
                term = pad_ref[start:start + tile, cols] * w_ref[j:j + 1, wcols]
                acc = term if acc is None else acc + term
            if is_ssm:
                acc = acc + sconv_b_ref[:, wcols]
            a = _silu(acc)
            if name in ("q", "k"):
                a = a * lax.rsqrt(jnp.sum(a * a, axis=-1, keepdims=True) + EPS)
                if name == "q":
                    a = a * (GDN_HEAD_DIM ** -0.5)
            act_ref[:, cols] = a
        off += width
    pad_ref[0:SUBLANES, :] = pad_ref[tile:tile + SUBLANES, :]

    lane = _lane_iota((tile, LANES))
    sm = small_ref[...]
    bias = prow_ref[0:1, :]
    neg_a = -jnp.exp(prow_ref[1:2, :])
    sp = _softplus(sm + bias)
    is_dt = lane < COL_BETA
    is_beta = (lane >= COL_BETA) & (lane < COL_G)
    is_g = (lane >= COL_G) & (lane < SPLIT_LANES)
    steps = jnp.where(is_dt | is_g, sp * neg_a, 0.0)
    sc = _dot(tri_ref[...], _compress3(steps))
    cs = sc + pltpu.roll(sc, LANES - SPLIT_LANES, 1) + pltpu.roll(sc, LANES - 2 * SPLIT_LANES, 1)
    cs = jnp.where(lane < SPLIT_LANES, cs, 0.0)
    beta = jnp.where(is_beta, 1.0 / (1.0 + jnp.exp(-sm)), 0.0)
    dt = jnp.where(is_dt, sp, 0.0)
    to_end = jnp.concatenate(
        [jnp.exp(cs[(c + 1) * CHUNK - 1:(c + 1) * CHUNK, :] - cs[c * CHUNK:(c + 1) * CHUNK, :]) for c in range(nchunk)],
        axis=0)
    wfac = jnp.where(is_dt, dt * to_end, 0.0)
    ea = jnp.where(is_dt, jnp.exp(cs), 0.0)
    acs = jnp.where(is_dt, cs, 0.0)
    gcs = jnp.where(is_g, cs, 0.0)
    exp_ref[:, EXP_DT:EXP_DT + SSM_WIDTH] = _dot(_compress3(dt), e_ssd_ref[...])
    exp_ref[:, EXP_W:EXP_W + SSM_WIDTH] = _dot(_compress3(wfac), e_ssd_ref[...])
    exp_ref[:, EXP_EA:EXP_EA + SSM_WIDTH] = _dot(_compress3(ea), e_ssd_ref[...])
    exp_ref[:, EXP_ACOL:EXP_ACOL + SSM_HEADS * LANES] = _dot(_compress3(acs), e_col_ref[...])
    exp_ref[:, EXP_BETA:EXP_BETA + GDN_WIDTH] = _dot(_compress3(beta), e_beta_ref[...])
    exp_ref[:, EXP_GCS:EXP_GCS + GDN_WIDTH] = _dot(_compress3(gcs), e_g_ref[...])

    xa = act_ref[:, 0:SSM_WIDTH]
    xdt_ref[...] = (xa * exp_ref[:, EXP_DT:EXP_DT + SSM_WIDTH]).astype(BF16)
    xw_ref[...] = (xa * exp_ref[:, EXP_W:EXP_W + SSM_WIDTH]).astype(BF16)

    row_i = lax.broadcasted_iota(jnp.int32, (CHUNK, CHUNK), 0)
    col_i = lax.broadcasted_iota(jnp.int32, (CHUNK, CHUNK), 1)
    incl = row_i >= col_i
    strict = row_i > col_i
    eye = (row_i == col_i).astype(F32)
    half = col_i < SSM_HEAD_DIM
    merge_masks = [(row_i // 2 == col_i // 2) & strict]
    size = 2
    while size < CHUNK:
        merge_masks.append((row_i // (2 * size) == col_i // (2 * size))
                           & ((row_i // size) % 2 == 1) & ((col_i // size) % 2 == 0))
        size *= 2
    act_b = SSM_WIDTH
    act_c = act_b + SSM_GROUPS * SSM_STATE
    act_q = act_c + SSM_GROUPS * SSM_STATE
    act_k = act_q + GDN_WIDTH
    act_v = act_k + GDN_WIDTH

    for c in range(nchunk):
        rows = slice(c * CHUNK, (c + 1) * CHUNK)
        last = slice((c + 1) * CHUNK - 1, (c + 1) * CHUNK)
        cs_t = cs[rows, :].T

        for g in range(SSM_GROUPS):
            gcols = slice(g * SSM_GROUP_WIDTH, (g + 1) * SSM_GROUP_WIDTH)
            bg = act_ref[rows, act_b + g * SSM_STATE:act_b + (g + 1) * SSM_STATE]
            cg = act_ref[rows, act_c + g * SSM_STATE:act_c + (g + 1) * SSM_STATE].astype(BF16)
            cb = _dot_nt(cg, bg.astype(BF16))
            ypairs = []
            for p in (2 * g, 2 * g + 1):
                xs_pair = xdt_ref[rows, p * LANES:(p + 1) * LANES]
                ys = []
                for hh in (2 * p, 2 * p + 1):
                    a_col = exp_ref[rows, EXP_ACOL + hh * LANES:EXP_ACOL + (hh + 1) * LANES]
                    a_row = cs_t[hh:hh + 1, :]
                    decay = jnp.exp(jnp.where(incl, a_col - a_row, NEG_BIG))
                    ys.append(_dot((cb * decay).astype(BF16), xs_pair))
                ypairs.append(jnp.where(half, ys[0], ys[1]))
            ea_g = exp_ref[rows, EXP_EA + g * SSM_GROUP_WIDTH:EXP_EA + (g + 1) * SSM_GROUP_WIDTH]
            h_prev = ht_ref[g]
            y_off = _dot(cg, h_prev.astype(BF16)) * ea_g
            y_ref[rows, gcols] = jnp.concatenate(ypairs, axis=1) + y_off
            ea_last = exp_ref[last, EXP_EA + g * SSM_GROUP_WIDTH:EXP_EA + (g + 1) * SSM_GROUP_WIDTH]
            ht_ref[g] = h_prev * ea_last + _dot(bg.T.astype(BF16), xw_ref[rows, gcols])

        for hd in range(GDN_HEADS):
            hcols = slice(hd * GDN_HEAD_DIM, (hd + 1) * GDN_HEAD_DIM)
            qh = act_ref[rows, act_q + hd * GDN_HEAD_DIM:act_q + (hd + 1) * GDN_HEAD_DIM]
            kh = act_ref[rows, act_k + hd * GDN_HEAD_DIM:act_k + (hd + 1) * GDN_HEAD_DIM]
            vh = act_ref[rows, act_v + hd * GDN_HEAD_DIM:act_v + (hd + 1) * GDN_HEAD_DIM]
            be = exp_ref[rows, EXP_BETA + hd * LANES:EXP_BETA + (hd + 1) * LANES]
            gc = exp_ref[rows, EXP_GCS + hd * LANES:EXP_GCS + (hd + 1) * LANES]
            g_last = exp_ref[last, EXP_GCS + hd * LANES:EXP_GCS + (hd + 1) * LANES]
            g_row = cs_t[COL_G + hd:COL_G + hd + 1, :]
            eg = jnp.exp(gc)
            kb = kh * be
            gram = _dot_nt(jnp.concatenate([kb, qh], axis=0).astype(BF16), kh.astype(BF16))
            decay = jnp.exp(jnp.where(incl, gc - g_row, NEG_BIG))
            a_mat = jnp.where(strict, gram[0:CHUNK, :] * decay, 0.0)
            qk = (gram[CHUNK:2 * CHUNK, :] * decay).astype(BF16)
            tinv = eye - jnp.where(merge_masks[0], a_mat, 0.0)
            for m in merge_masks[1:]:
                tinv_b = tinv.astype(BF16)
                inner = _dot(jnp.where(m, a_mat, 0.0).astype(BF16), tinv_b)
                tinv = tinv - _dot(tinv_b, inner.astype(BF16))
            rhs = jnp.concatenate([vh * be, kb * eg], axis=1).astype(BF16)
            uw = _dot(tinv.astype(BF16), rhs)
            s_prev = s_ref[hd]
            wq = _dot(jnp.concatenate([uw[:, GDN_HEAD_DIM:], qh * eg], axis=0).astype(BF16), s_prev.astype(BF16))
            v_new = (uw[:, 0:GDN_HEAD_DIM] - wq[0:CHUNK, :]).astype(BF16)
            o_ref[rows, hcols] = wq[CHUNK:2 * CHUNK, :] + _dot(qk, v_new)
            k_dec_t = (kh * jnp.exp(g_last - gc)).T.astype(BF16)
            s_ref[hd] = s_prev * jnp.exp(g_last) + _dot(k_dec_t, v_new)

    for g in range(SSM_GROUPS):
        gcols = slice(g * SSM_GROUP_WIDTH, (g + 1) * SSM_GROUP_WIDTH)
        y = (y_ref[:, gcols] + drow_ref[:, gcols] * act_ref[:, gcols]) * _silu(z_ref[:, gcols].astype(F32))
        ms = jnp.mean(y * y, axis=-1, keepdims=True)
        out_ref[:, gcols] = (y * lax.rsqrt(ms + EPS) * snorm_ref[:, gcols]).astype(out_ref.dtype)
    for hd in range(GDN_HEADS):
        hcols = slice(hd * GDN_HEAD_DIM, (hd + 1) * GDN_HEAD_DIM)
        o = o_ref[:, hcols]
        ms = jnp.mean(o * o, axis=-1, keepdims=True)
        o = o * lax.rsqrt(ms + EPS) * gnorm_ref[:, hcols] * _silu(gate_ref[:, hcols].astype(F32))
        out_ref[:, SSM_WIDTH + hd * GDN_HEAD_DIM:SSM_WIDTH + (hd + 1) * GDN_HEAD_DIM] = o.astype(out_ref.dtype)


def _expansion_matrix(first_row, heads, lanes_per_head):
    r = jnp.arange(LANES)[:, None]
    col_head = jnp.arange(heads * lanes_per_head)[None, :] // lanes_per_head
    src = (r % SPLIT_LANES) - first_row
    valid = r < 3 * SPLIT_LANES
    return ((src == col_head) & valid).astype(BF16)


def _mixer(streams, bsz, seqlen, sconv_w, sconv_b, gconv_w, prow, drow, snorm, gnorm):
    tile = min(T_MIX, seqlen)
    nt = seqlen // tile
    r = jnp.arange(tile)
    tri = ((r[:, None] >= r[None, :]) & (r[:, None] // CHUNK == r[None, :] // CHUNK)).astype(BF16)
    e_ssd = _expansion_matrix(COL_DT, SSM_HEADS, SSM_HEAD_DIM)
    e_col = _expansion_matrix(COL_DT, SSM_HEADS, LANES)
    e_beta = _expansion_matrix(COL_BETA, GDN_HEADS, GDN_HEAD_DIM)
    e_g = _expansion_matrix(COL_G, GDN_HEADS, GDN_HEAD_DIM)

    def tok_spec(width):
        return pl.BlockSpec((None, tile, width), lambda b, t: (b, t, 0))

    def const_spec(arr):
        return pl.BlockSpec(arr.shape, lambda b, t: (0,) * arr.ndim)

    stream_arrays = [streams[name].reshape(bsz, seqlen, width) for name, width in STREAMS]
    consts = [sconv_w, sconv_b, gconv_w, prow, drow, snorm, gnorm, tri, e_ssd, e_col, e_beta, e_g]
    return pl.pallas_call(
        functools.partial(_mixer_body, tile=tile),
        grid=(bsz, nt),
        in_specs=[tok_spec(w) for _, w in STREAMS] + [const_spec(a) for a in consts],
        out_specs=tok_spec(MIX_WIDTH),
        out_shape=jax.ShapeDtypeStruct((bsz, seqlen, MIX_WIDTH), BF16),
        scratch_shapes=[
            pltpu.VMEM((tile + SUBLANES, CONV_TOTAL), F32),
            pltpu.VMEM((tile, CONV_TOTAL), F32),
            pltpu.VMEM((tile, EXP_TOTAL), F32),
            pltpu.VMEM((tile, SSM_WIDTH), BF16),
            pltpu.VMEM((tile, SSM_WIDTH), BF16),
            pltpu.VMEM((tile, SSM_WIDTH), F32),
            pltpu.VMEM((tile, GDN_WIDTH), F32),
            pltpu.VMEM((SSM_GROUPS, SSM_STATE, SSM_GROUP_WIDTH), F32),
            pltpu.VMEM((GDN_HEADS, GDN_HEAD_DIM, GDN_HEAD_DIM), F32),
        ],
        compiler_params=pltpu.CompilerParams(dimension_semantics=("arbitrary", "arbitrary"),
                                             vmem_limit_bytes=VMEM_LIMIT_BYTES),
        name="mixer",
    )(*stream_arrays, *consts)


def _rearranged_in_proj_weight(w_in):
    small = jnp.concatenate([w_in[:, OFF_DT:OFF_DT + SSM_HEADS], w_in[:, OFF_BETA:OFF_BETA + GDN_HEADS],
                             w_in[:, OFF_A:OFF_A + GDN_HEADS],
                             jnp.zeros((D_MODEL, LANES - SPLIT_LANES), w_in.dtype)], axis=1)
    parts = [w_in[:, OFF_Z:OFF_Z + SSM_WIDTH], w_in[:, OFF_X:OFF_X + SSM_WIDTH],
             w_in[:, OFF_B:OFF_B + 512], w_in[:, OFF_C:OFF_C + 512],
             w_in[:, OFF_Q:OFF_Q + GDN_WIDTH], w_in[:, OFF_K:OFF_K + GDN_WIDTH], w_in[:, OFF_V:OFF_V + GDN_WIDTH],
             w_in[:, OFF_GATE:OFF_GATE + GDN_WIDTH], small]
    return jnp.concatenate(parts, axis=1).astype(BF16)


def _scalar_rows(ssm_dt_bias, ssm_a_log, gdn_dt_bias, gdn_a_log):
    zeros8 = jnp.zeros((GDN_HEADS,), F32)
    tail = jnp.zeros((LANES - SPLIT_LANES,), F32)
    bias = jnp.concatenate([ssm_dt_bias.astype(F32), zeros8, gdn_dt_bias.astype(F32), tail])
    alog = jnp.concatenate([ssm_a_log.astype(F32), zeros8, gdn_a_log.astype(F32), tail])
    return jnp.concatenate([bias[None], alog[None], jnp.zeros((SUBLANES - 2, LANES), F32)], axis=0)


def kernel(x, attn_norm_w, w_in, ssm_conv_w, ssm_conv_b, ssm_dt_bias, ssm_a_log, ssm_d, ssm_norm_w, gdn_conv_w,
           gdn_dt_bias, gdn_a_log, gdn_norm_w, w_out, mlp_norm_w, w_up, w_down, final_norm_w):
    bsz, seqlen, _ = x.shape
    depth = w_in.shape[0]
    assert seqlen % CHUNK == 0 and (bsz * seqlen) % min(TM_PROJ, bsz * seqlen) == 0
    h = x.reshape(bsz * seqlen, D_MODEL).astype(F32)
    final_w = final_norm_w.reshape(1, D_MODEL).astype(F32)
    for layer in range(depth):
        outs = _in_proj(h, attn_norm_w[layer].reshape(1, D_MODEL).astype(F32), _rearranged_in_proj_weight(w_in[layer]))
        streams = {name: o for (name, _), o in zip(STREAMS, outs)}
        mixed = _mixer(
            streams, bsz, seqlen,
            ssm_conv_w[layer].astype(F32), ssm_conv_b[layer].reshape(1, -1).astype(F32), gdn_conv_w[layer].astype(F32),
            _scalar_rows(ssm_dt_bias[layer], ssm_a_log[layer], gdn_dt_bias[layer], gdn_a_log[layer]),
            jnp.repeat(ssm_d[layer].astype(F32), SSM_HEAD_DIM).reshape(1, SSM_WIDTH),
            ssm_norm_w[layer].reshape(1, SSM_WIDTH).astype(F32),
            jnp.tile(gdn_norm_w[layer].astype(F32), GDN_HEADS).reshape(1, GDN_WIDTH))
        h = _out_mlp(mixed.reshape(bsz * seqlen, MIX_WIDTH), h, w_out[layer].astype(BF16),
                     mlp_norm_w[layer].reshape(1, D_MODEL).astype(F32), w_up[layer].astype(BF16),
                     w_down[layer].astype(BF16), final_w, final=(layer == depth - 1))
    return h.reshape(bsz, seqlen, D_MODEL).astype(x.dtype)
```

```python
import functools

import jax
import jax.numpy as jnp
from jax import lax
from jax.experimental import pallas as pl
from jax.experimental.pallas import tpu as pltpu

F32 = jnp.float32
BF16 = jnp.bfloat16

D_MODEL = 1024
SSM_HEADS = 16
SSM_HEAD_DIM = 64
SSM_WIDTH = SSM_HEADS * SSM_HEAD_DIM
SSM_GROUPS = 4
SSM_STATE = 128
SSM_GROUP_WIDTH = SSM_WIDTH // SSM_GROUPS
GDN_HEADS = 8
GDN_HEAD_DIM = 128
GDN_WIDTH = GDN_HEADS * GDN_HEAD_DIM
CONV_WIDTH = 4
MIX_WIDTH = SSM_WIDTH + GDN_WIDTH
D_FF = 4 * D_MODEL
EPS = 1e-6

OFF_Z = 0
OFF_X = SSM_WIDTH
OFF_B = OFF_X + SSM_WIDTH
OFF_C = OFF_B + SSM_GROUPS * SSM_STATE
OFF_DT = OFF_C + SSM_GROUPS * SSM_STATE
OFF_Q = OFF_DT + SSM_HEADS
OFF_K = OFF_Q + GDN_WIDTH
OFF_V = OFF_K + GDN_WIDTH
OFF_GATE = OFF_V + GDN_WIDTH
OFF_BETA = OFF_GATE + GDN_WIDTH
OFF_A = OFF_BETA + GDN_HEADS
IN_PROJ = OFF_A + GDN_HEADS

LANES = 128
SUBLANES = 8
CHUNK = 128
COL_DT = 0
COL_BETA = SSM_HEADS
COL_G = SSM_HEADS + GDN_HEADS
SPLIT_LANES = 32
STREAMS = (("z", SSM_WIDTH), ("x", SSM_WIDTH), ("b", SSM_GROUPS * SSM_STATE), ("c", SSM_GROUPS * SSM_STATE),
           ("q", GDN_WIDTH), ("k", GDN_WIDTH), ("v", GDN_WIDTH), ("gate", GDN_WIDTH), ("small", LANES))
IN_PROJ_PAD = sum(w for _, w in STREAMS)
CONV_STREAMS = (("x", SSM_WIDTH), ("b", 512), ("c", 512), ("q", GDN_WIDTH), ("k", GDN_WIDTH), ("v", GDN_WIDTH))
CONV_TOTAL = sum(w for _, w in CONV_STREAMS)
NEG_BIG = -1e30
VMEM_LIMIT_BYTES = 56 * 1024 * 1024

TM_PROJ = 512
T_MIX = 256


def _silu(x):
    return x / (1.0 + jnp.exp(-x))


def _softplus(x):
    return jnp.maximum(x, 0.0) + jnp.log1p(jnp.exp(-jnp.abs(x)))


def _dot(a, b):
    return jnp.dot(a, b, preferred_element_type=F32)


def _dot_nt(a, b):
    return lax.dot_general(a, b, (((1,), (1,)), ((), ())), preferred_element_type=F32)


def _compress3(y):
    hi = y.astype(BF16).astype(F32)
    r1 = y - hi
    mid = r1.astype(BF16).astype(F32)
    lo = r1 - mid
    return (hi + pltpu.roll(mid, SPLIT_LANES, 1) + pltpu.roll(lo, 2 * SPLIT_LANES, 1)).astype(BF16)


def _lane_iota(shape):
    return lax.broadcasted_iota(jnp.int32, shape, len(shape) - 1)


def _in_proj_body(h_ref, nw_ref, w_ref, *out_refs):
    h = h_ref[...]
    ms = jnp.mean(h * h, axis=-1, keepdims=True)
    hn = (h * lax.rsqrt(ms + EPS) * nw_ref[...]).astype(BF16)
    off = 0
    for ref, (_, width) in zip(out_refs, STREAMS):
        ref[...] = _dot(hn, w_ref[:, off:off + width]).astype(ref.dtype)
        off += width


def _in_proj(h2d, norm_w, w_r):
    n = h2d.shape[0]
    tm = min(TM_PROJ, n)
    grid = (n // tm,)
    out_shape = tuple(jax.ShapeDtypeStruct((n, w), F32 if name == "small" else BF16) for name, w in STREAMS)
    out_specs = tuple(pl.BlockSpec((tm, w), lambda i: (i, 0)) for _, w in STREAMS)
    return pl.pallas_call(
        _in_proj_body,
        grid=grid,
        in_specs=[
            pl.BlockSpec((tm, D_MODEL), lambda i: (i, 0)),
            pl.BlockSpec((1, D_MODEL), lambda i: (0, 0)),
            pl.BlockSpec((D_MODEL, IN_PROJ_PAD), lambda i: (0, 0), pipeline_mode=pl.Buffered(1)),
        ],
        out_specs=out_specs,
        out_shape=out_shape,
        compiler_params=pltpu.CompilerParams(dimension_semantics=("arbitrary",),
                                             vmem_limit_bytes=VMEM_LIMIT_BYTES),
        name="in_proj",
    )(h2d, norm_w, w_r)


def _out_mlp_body(mixed_ref, h_ref, wout_ref, nw_ref, wup_ref, wdown_ref, fnw_ref, o_ref, *, final):
    h1 = h_ref[...] + _dot(mixed_ref[...], wout_ref[...])
    ms = jnp.mean(h1 * h1, axis=-1, keepdims=True)
    hn = (h1 * lax.rsqrt(ms + EPS) * nw_ref[...]).astype(BF16)
    acc = h1
    for j in range(D_FF // D_MODEL):
        cols = slice(j * D_MODEL, (j + 1) * D_MODEL)
        up = _dot(hn, wup_ref[:, cols])
        act = jnp.square(jnp.maximum(up, 0.0)).astype(BF16)
        acc = acc + _dot(act, wdown_ref[cols, :])
    if final:
        ms2 = jnp.mean(acc * acc, axis=-1, keepdims=True)
        acc = acc * lax.rsqrt(ms2 + EPS) * fnw_ref[...]
    o_ref[...] = acc


def _out_mlp(mixed, h2d, w_out, norm_w, w_up, w_down, final_w, final):
    n = h2d.shape[0]
    tm = min(TM_PROJ, n)
    const = dict(pipeline_mode=pl.Buffered(1))
    return pl.pallas_call(
        functools.partial(_out_mlp_body, final=final),
        grid=(n // tm,),
        in_specs=[
            pl.BlockSpec((tm, MIX_WIDTH), lambda i: (i, 0)),
            pl.BlockSpec((tm, D_MODEL), lambda i: (i, 0)),
            pl.BlockSpec((MIX_WIDTH, D_MODEL), lambda i: (0, 0), **const),
            pl.BlockSpec((1, D_MODEL), lambda i: (0, 0)),
            pl.BlockSpec((D_MODEL, D_FF), lambda i: (0, 0), **const),
            pl.BlockSpec((D_FF, D_MODEL), lambda i: (0, 0), **const),
            pl.BlockSpec((1, D_MODEL), lambda i: (0, 0)),
        ],
        out_specs=pl.BlockSpec((tm, D_MODEL), lambda i: (i, 0)),
        out_shape=jax.ShapeDtypeStruct((n, D_MODEL), F32),
        compiler_params=pltpu.CompilerParams(dimension_semantics=("arbitrary",),
                                             vmem_limit_bytes=VMEM_LIMIT_BYTES),
        name="out_mlp",
    )(mixed, h2d, w_out, norm_w, w_up, w_down, final_w)


EXP_DT = 0
EXP_W = EXP_DT + SSM_WIDTH
EXP_EA = EXP_W + SSM_WIDTH
EXP_ACOL = EXP_EA + SSM_WIDTH
EXP_BETA = EXP_ACOL + SSM_HEADS * LANES
EXP_GCS = EXP_BETA + GDN_WIDTH
EXP_TOTAL = EXP_GCS + GDN_WIDTH


def _mixer_body(z_ref, x_ref, b_ref, c_ref, q_ref, k_ref, v_ref, gate_ref, small_ref,
                sconv_w_ref, sconv_b_ref, gconv_w_ref, prow_ref, drow_ref, snorm_ref, gnorm_ref,
                tri_ref, e_ssd_ref, e_col_ref, e_beta_ref, e_g_ref,
                out_ref,
                pad_ref, act_ref, exp_ref, xdt_ref, xw_ref, y_ref, o_ref, ht_ref, s_ref, *, tile):
    nchunk = tile // CHUNK
    t_idx = pl.program_id(1)

    @pl.when(t_idx == 0)
    def _():
        pad_ref[0:SUBLANES, :] = jnp.zeros((SUBLANES, CONV_TOTAL), F32)
        ht_ref[...] = jnp.zeros_like(ht_ref)
        s_ref[...] = jnp.zeros_like(s_ref)

    raw_refs = dict(x=x_ref, b=b_ref, c=c_ref, q=q_ref, k=k_ref, v=v_ref)
    off = 0
    for name, width in CONV_STREAMS:
        is_ssm = name in ("x", "b", "c")
        w_ref = sconv_w_ref if is_ssm else gconv_w_ref
        w_off = off if is_ssm else off - 2 * SSM_WIDTH
        step = LANES if name in ("q", "k") else 512
        for c0 in range(0, width, step):
            cols = slice(off + c0, off + c0 + step)
            wcols = slice(w_off + c0, w_off + c0 + step)
            pad_ref[SUBLANES:SUBLANES + tile, cols] = raw_refs[name][:, c0:c0 + step].astype(F32)
            acc = None
            for j in range(CONV_WIDTH):
                start = SUBLANES - (CONV_WIDTH - 1) + j
                term = pad_ref[start:start + tile, cols] * w_ref[j:j + 1, wcols]
                acc = term if acc is None else acc + term
            if is_ssm:
                acc = acc + sconv_b_ref[:, wcols]
            a = _silu(acc)
            if name in ("q", "k"):
                a = a * lax.rsqrt(jnp.sum(a * a, axis=-1, keepdims=True) + EPS)
                if name == "q":
                    a = a * (GDN_HEAD_DIM ** -0.5)
            act_ref[:, cols] = a
        off += width
    pad_ref[0:SUBLANES, :] = pad_ref[tile:tile + SUBLANES, :]

    lane = _lane_iota((tile, LANES))
    sm = small_ref[...]
    bias = prow_ref[0:1, :]
    neg_a = -jnp.exp(prow_ref[1:2, :])
    sp = _softplus(sm + bias)
    is_dt = lane < COL_BETA
    is_beta = (lane >= COL_BETA) & (lane < COL_G)
    is_g = (lane >= COL_G) & (lane < SPLIT_LANES)
    steps = jnp.where(is_dt | is_g, sp * neg_a, 0.0)
    sc = _dot(tri_ref[...], _compress3(steps))
    cs = sc + pltpu.roll(sc, LANES - SPLIT_LANES, 1) + pltpu.roll(sc, LANES - 2 * SPLIT_LANES, 1)
    cs = jnp.where(lane < SPLIT_LANES, cs, 0.0)
    beta = jnp.where(is_beta, 1.0 / (1.0 + jnp.exp(-sm)), 0.0)
    dt = jnp.where(is_dt, sp, 0.0)
    to_end = jnp.concatenate(
        [jnp.exp(cs[(c + 1) * CHUNK - 1:(c + 1) * CHUNK, :] - cs[c * CHUNK:(c + 1) * CHUNK, :]) for c in range(nchunk)],
        axis=0)
    wfac = jnp.where(is_dt, dt * to_end, 0.0)
    ea = jnp.where(is_dt, jnp.exp(cs), 0.0)
    acs = jnp.where(is_dt, cs, 0.0)
    gcs = jnp.where(is_g, cs, 0.0)
    exp_ref[:, EXP_DT:EXP_DT + SSM_WIDTH] = _dot(_compress3(dt), e_ssd_ref[...])
    exp_ref[:, EXP_W:EXP_W + SSM_WIDTH] = _dot(_compress3(wfac), e_ssd_ref[...])
    exp_ref[:, EXP_EA:EXP_EA + SSM_WIDTH] = _dot(_compress3(ea), e_ssd_ref[...])
    exp_ref[:, EXP_ACOL:EXP_ACOL + SSM_HEADS * LANES] = _dot(_compress3(acs), e_col_ref[...])
    exp_ref[:, EXP_BETA:EXP_BETA + GDN_WIDTH] = _dot(_compress3(beta), e_beta_ref[...])
    exp_ref[:, EXP_GCS:EXP_GCS + GDN_WIDTH] = _dot(_compress3(gcs), e_g_ref[...])

    xa = act_ref[:, 0:SSM_WIDTH]
    xdt_ref[...] = (xa * exp_ref[:, EXP_DT:EXP_DT + SSM_WIDTH]).astype(BF16)
    xw_ref[...] = (xa * exp_ref[:, EXP_W:EXP_W + SSM_WIDTH]).astype(BF16)

    row_i = lax.broadcasted_iota(jnp.int32, (CHUNK, CHUNK), 0)
    col_i = lax.broadcasted_iota(jnp.int32, (CHUNK, CHUNK), 1)
    incl = row_i >= col_i
    strict = row_i > col_i
    eye = (row_i == col_i).astype(F32)
    half = col_i < SSM_HEAD_DIM
    merge_masks = [(row_i // 2 == col_i // 2) & strict]
    size = 2
    while size < CHUNK:
        merge_masks.append((row_i // (2 * size) == col_i // (2 * size))
                           & ((row_i // size) % 2 == 1) & ((col_i // size) % 2 == 0))
        size *= 2
    act_b = SSM_WIDTH
    act_c = act_b + SSM_GROUPS * SSM_STATE
    act_q = act_c + SSM_GROUPS * SSM_STATE
    act_k = act_q + GDN_WIDTH
    act_v = act_k + GDN_WIDTH

    chunk_rows = [slice(c * CHUNK, (c + 1) * CHUNK) for c in range(nchunk)]
    chunk_last = [slice((c + 1) * CHUNK - 1, (c + 1) * CHUNK) for c in range(nchunk)]
    cs_t = [cs[rows, :].T for rows in chunk_rows]


    for c in range(nchunk):
        rows, last = chunk_rows[c], chunk_last[c]
        bgs = [act_ref[rows, act_b + g * SSM_STATE:act_b + (g + 1) * SSM_STATE] for g in range(SSM_GROUPS)]
        cgs = [act_ref[rows, act_c + g * SSM_STATE:act_c + (g + 1) * SSM_STATE].astype(BF16)
               for g in range(SSM_GROUPS)]
        cbs = [_dot_nt(cgs[g], bgs[g].astype(BF16)) for g in range(SSM_GROUPS)]
        y_heads = []
        for hh in range(SSM_HEADS):
            a_col = exp_ref[rows, EXP_ACOL + hh * LANES:EXP_ACOL + (hh + 1) * LANES]
            a_row = cs_t[c][hh:hh + 1, :]
            decay = jnp.exp(jnp.where(incl, a_col - a_row, NEG_BIG))
            pair = hh // 2
            y_heads.append(_dot((cbs[hh // (SSM_HEADS // SSM_GROUPS)] * decay).astype(BF16),
                                xdt_ref[rows, pair * LANES:(pair + 1) * LANES]))
        for g in range(SSM_GROUPS):
            gcols = slice(g * SSM_GROUP_WIDTH, (g + 1) * SSM_GROUP_WIDTH)
            ea_g = exp_ref[rows, EXP_EA + g * SSM_GROUP_WIDTH:EXP_EA + (g + 1) * SSM_GROUP_WIDTH]
            ea_last = exp_ref[last, EXP_EA + g * SSM_GROUP_WIDTH:EXP_EA + (g + 1) * SSM_GROUP_WIDTH]
            h_prev = ht_ref[g]
            y_off = _dot(cgs[g], h_prev.astype(BF16)) * ea_g
            ypairs = [jnp.where(half, y_heads[2 * p], y_heads[2 * p + 1]) for p in (2 * g, 2 * g + 1)]
            y_ref[rows, gcols] = jnp.concatenate(ypairs, axis=1) + y_off
            ht_ref[g] = h_prev * ea_last + _dot(bgs[g].T.astype(BF16), xw_ref[rows, gcols])

    units = [(c, hd) for c in range(nchunk) for hd in range(GDN_HEADS)]
    a_mats, qks, rhss, qdecs, kdec_ts, s_decays = {}, {}, {}, {}, {}, {}
    for u in units:
        c, hd = u
        rows, last = chunk_rows[c], chunk_last[c]
        qh = act_ref[rows, act_q + hd * GDN_HEAD_DIM:act_q + (hd + 1) * GDN_HEAD_DIM]
        kh = act_ref[rows, act_k + hd * GDN_HEAD_DIM:act_k + (hd + 1) * GDN_HEAD_DIM]
        vh = act_ref[rows, act_v + hd * GDN_HEAD_DIM:act_v + (hd + 1) * GDN_HEAD_DIM]
        be = exp_ref[rows, EXP_BETA + hd * LANES:EXP_BETA + (hd + 1) * LANES]
        gc = exp_ref[rows, EXP_GCS + hd * LANES:EXP_GCS + (hd + 1) * LANES]
        g_last = exp_ref[last, EXP_GCS + hd * LANES:EXP_GCS + (hd + 1) * LANES]
        g_row = cs_t[c][COL_G + hd:COL_G + hd + 1, :]
        eg = jnp.exp(gc)
        kb = kh * be
        gram = _dot_nt(jnp.concatenate([kb, qh], axis=0).astype(BF16), kh.astype(BF16))
        decay = jnp.exp(jnp.where(incl, gc - g_row, NEG_BIG))
        a_mats[u] = jnp.where(strict, gram[0:CHUNK, :] * decay, 0.0)
        qks[u] = (gram[CHUNK:2 * CHUNK, :] * decay).astype(BF16)
        rhss[u] = jnp.concatenate([vh * be, kb * eg], axis=1).astype(BF16)
        qdecs[u] = (qh * eg).astype(BF16)
        kdec_ts[u] = (kh * jnp.exp(g_last - gc)).T.astype(BF16)
        s_decays[u] = jnp.exp(g_last)

    tinv = {u: eye - jnp.where(merge_masks[0], a_mats[u], 0.0) for u in units}
    for m in merge_masks[1:]:
        tinv_b = {u: tinv[u].astype(BF16) for u in units}
        inner = {u: _dot(jnp.where(m, a_mats[u], 0.0).astype(BF16), tinv_b[u]) for u in units}
        tinv = {u: tinv[u] - _dot(tinv_b[u], inner[u].astype(BF16)) for u in units}
    uws = {u: _dot(tinv[u].astype(BF16), rhss[u]) for u in units}

    states = [s_ref[hd] for hd in range(GDN_HEADS)]
    for c in range(nchunk):
        rows = chunk_rows[c]
        wqs = [_dot(jnp.concatenate([uws[(c, hd)][:, GDN_HEAD_DIM:].astype(BF16), qdecs[(c, hd)]], axis=0),
                    states[hd].astype(BF16)) for hd in range(GDN_HEADS)]
        v_news = [(uws[(c, hd)][:, 0:GDN_HEAD_DIM] - wqs[hd][0:CHUNK, :]).astype(BF16) for hd in range(GDN_HEADS)]
        for hd in range(GDN_HEADS):
            o_ref[rows, hd * GDN_HEAD_DIM:(hd + 1) * GDN_HEAD_DIM] = (
                wqs[hd][CHUNK:2 * CHUNK, :] + _dot(qks[(c, hd)], v_news[hd]))
        states = [states[hd] * s_decays[(c, hd)] + _dot(kdec_ts[(c, hd)], v_news[hd]) for hd in range(GDN_HEADS)]
    for hd in range(GDN_HEADS):
        s_ref[hd] = states[hd]

    for g in range(SSM_GROUPS):
        gcols = slice(g * SSM_GROUP_WIDTH, (g + 1) * SSM_GROUP_WIDTH)
        y = (y_ref[:, gcols] + drow_ref[:, gcols] * act_ref[:, gcols]) * _silu(z_ref[:, gcols].astype(F32))
        ms = jnp.mean(y * y, axis=-1, keepdims=True)
        out_ref[:, gcols] = (y * lax.rsqrt(ms + EPS) * snorm_ref[:, gcols]).astype(out_ref.dtype)
    for hd in range(GDN_HEADS):
        hcols = slice(hd * GDN_HEAD_DIM, (hd + 1) * GDN_HEAD_DIM)
        o = o_ref[:, hcols]
        ms = jnp.mean(o * o, axis=-1, keepdims=True)
        o = o * lax.rsqrt(ms + EPS) * gnorm_ref[:, hcols] * _silu(gate_ref[:, hcols].astype(F32))
        out_ref[:, SSM_WIDTH + hd * GDN_HEAD_DIM:SSM_WIDTH + (hd + 1) * GDN_HEAD_DIM] = o.astype(out_ref.dtype)


def _expansion_matrix(first_row, heads, lanes_per_head):
    r = jnp.arange(LANES)[:, None]
    col_head = jnp.arange(heads * lanes_per_head)[None, :] // lanes_per_head
    src = (r % SPLIT_LANES) - first_row
    valid = r < 3 * SPLIT_LANES
    return ((src == col_head) & valid).astype(BF16)


def _mixer(streams, bsz, seqlen, sconv_w, sconv_b, gconv_w, prow, drow, snorm, gnorm):
    tile = min(T_MIX, seqlen)
    nt = seqlen // tile
    r = jnp.arange(tile)
    tri = ((r[:, None] >= r[None, :]) & (r[:, None] // CHUNK == r[None, :] // CHUNK)).astype(BF16)
    e_ssd = _expansion_matrix(COL_DT, SSM_HEADS, SSM_HEAD_DIM)
    e_col = _expansion_matrix(COL_DT, SSM_HEADS, LANES)
    e_beta = _expansion_matrix(COL_BETA, GDN_HEADS, GDN_HEAD_DIM)
    e_g = _expansion_matrix(COL_G, GDN_HEADS, GDN_HEAD_DIM)

    def tok_spec(width):
        return pl.BlockSpec((None, tile, width), lambda b, t: (b, t, 0))

    def const_spec(arr):
        return pl.BlockSpec(arr.shape, lambda b, t: (0,) * arr.ndim)

    stream_arrays = [streams[name].reshape(bsz, seqlen, width) for name, width in STREAMS]
    consts = [sconv_w, sconv_b, gconv_w, prow, drow, snorm, gnorm, tri, e_ssd, e_col, e_beta, e_g]
    return pl.pallas_call(
        functools.partial(_mixer_body, tile=tile),
        grid=(bsz, nt),
        in_specs=[tok_spec(w) for _, w in STREAMS] + [const_spec(a) for a in consts],
        out_specs=tok_spec(MIX_WIDTH),
        out_shape=jax.ShapeDtypeStruct((bsz, seqlen, MIX_WIDTH), BF16),
        scratch_shapes=[
            pltpu.VMEM((tile + SUBLANES, CONV_TOTAL), F32),
            pltpu.VMEM((tile, CONV_TOTAL), F32),
            pltpu.VMEM((tile, EXP_TOTAL), F32),
            pltpu.VMEM((tile, SSM_WIDTH), BF16),
            pltpu.VMEM((tile, SSM_WIDTH), BF16),
            pltpu.VMEM((tile, SSM_WIDTH), F32),
            pltpu.VMEM((tile, GDN_WIDTH), F32),
            pltpu.VMEM((SSM_GROUPS, SSM_STATE, SSM_GROUP_WIDTH), F32),
            pltpu.VMEM((GDN_HEADS, GDN_HEAD_DIM, GDN_HEAD_DIM), F32),
        ],
        compiler_params=pltpu.CompilerParams(dimension_semantics=("arbitrary", "arbitrary"),
                                             vmem_limit_bytes=VMEM_LIMIT_BYTES),
        name="mixer",
    )(*stream_arrays, *consts)


def _rearranged_in_proj_weight(w_in):
    small = jnp.concatenate([w_in[:, OFF_DT:OFF_DT + SSM_HEADS], w_in[:, OFF_BETA:OFF_BETA + GDN_HEADS],
                             w_in[:, OFF_A:OFF_A + GDN_HEADS],
                             jnp.zeros((D_MODEL, LANES - SPLIT_LANES), w_in.dtype)], axis=1)
    parts = [w_in[:, OFF_Z:OFF_Z + SSM_WIDTH], w_in[:, OFF_X:OFF_X + SSM_WIDTH],
             w_in[:, OFF_B:OFF_B + 512], w_in[:, OFF_C:OFF_C + 512],
             w_in[:, OFF_Q:OFF_Q + GDN_WIDTH], w_in[:, OFF_K:OFF_K + GDN_WIDTH], w_in[:, OFF_V:OFF_V + GDN_WIDTH],
             w_in[:, OFF_GATE:OFF_GATE + GDN_WIDTH], small]
    return jnp.concatenate(parts, axis=1).astype(BF16)


def _scalar_rows(ssm_dt_bias, ssm_a_log, gdn_dt_bias, gdn_a_log):
    zeros8 = jnp.zeros((GDN_HEADS,), F32)
    tail = jnp.zeros((LANES - SPLIT_LANES,), F32)
    bias = jnp.concatenate([ssm_dt_bias.astype(F32), zeros8, gdn_dt_bias.astype(F32), tail])
    alog = jnp.concatenate([ssm_a_log.astype(F32), zeros8, gdn_a_log.astype(F32), tail])
    return jnp.concatenate([bias[None], alog[None], jnp.zeros((SUBLANES - 2, LANES), F32)], axis=0)


def kernel(x, attn_norm_w, w_in, ssm_conv_w, ssm_conv_b, ssm_dt_bias, ssm_a_log, ssm_d, ssm_norm_w, gdn_conv_w,
           gdn_dt_bias, gdn_a_log, gdn_norm_w, w_out, mlp_norm_w, w_up, w_down, final_norm_w):
    bsz, seqlen, _ = x.shape
    depth = w_in.shape[0]
    assert seqlen % CHUNK == 0 and (bsz * seqlen) % min(TM_PROJ, bsz * seqlen) == 0
    h = x.reshape(bsz * seqlen, D_MODEL).astype(F32)
    final_w = final_norm_w.reshape(1, D_MODEL).astype(F32)
    for layer in range(depth):
        outs = _in_proj(h, attn_norm_w[layer].reshape(1, D_MODEL).astype(F32), _rearranged_in_proj_weight(w_in[layer]))
        streams = {name: o for (name, _), o in zip(STREAMS, outs)}
        mixed = _mixer(
            streams, bsz, seqlen,
            ssm_conv_w[layer].astype(F32), ssm_conv_b[layer].reshape(1, -1).astype(F32), gdn_conv_w[layer].astype(F32),
            _scalar_rows(ssm_dt_bias[layer], ssm_a_log[layer], gdn_dt_bias[layer], gdn_a_log[layer]),
            jnp.repeat(ssm_d[layer].astype(F32), SSM_HEAD_DIM).reshape(1, SSM_WIDTH),
            ssm_norm_w[layer].reshape(1, SSM_WIDTH).astype(F32),
            jnp.tile(gdn_norm_w[layer].astype(F32), GDN_HEADS).reshape(1, GDN_WIDTH))
        h = _out_mlp(mixed.reshape(bsz * seqlen, MIX_WIDTH), h, w_out[layer].astype(BF16),
                     mlp_norm_w[layer].reshape(1, D_MODEL).astype(F32), w_up[layer].astype(BF16),
                     w_down[layer].astype(BF16), final_w, final=(layer == depth - 1))
    return h.reshape(bsz, seqlen, D_MODEL).astype(x.dtype)
```

```python
import functools

import jax
import jax.numpy as jnp
from jax import lax
from jax.experimental import pallas as pl
from jax.experimental.pallas import tpu as pltpu

F32 = jnp.float32
BF16 = jnp.bfloat16

D_MODEL = 1024
SSM_HEADS = 16
SSM_HEAD_DIM = 64
SSM_WIDTH = SSM_HEADS * SSM_HEAD_DIM
SSM_GROUPS = 4
SSM_STATE = 128
SSM_BC_WIDTH = SSM_GROUPS * SSM_STATE
SSM_GROUP_WIDTH = SSM_WIDTH // SSM_GROUPS
GDN_HEADS = 8
GDN_HEAD_DIM = 128
GDN_WIDTH = GDN_HEADS * GDN_HEAD_DIM
CONV_WIDTH = 4
MIX_WIDTH = SSM_WIDTH + GDN_WIDTH
D_FF = 4 * D_MODEL
EPS = 1e-6

OFF_Z = 0
OFF_X = SSM_WIDTH
OFF_B = OFF_X + SSM_WIDTH
OFF_C = OFF_B + SSM_BC_WIDTH
OFF_DT = OFF_C + SSM_BC_WIDTH
OFF_Q = OFF_DT + SSM_HEADS
OFF_K = OFF_Q + GDN_WIDTH
OFF_V = OFF_K + GDN_WIDTH
OFF_GATE = OFF_V + GDN_WIDTH
OFF_BETA = OFF_GATE + GDN_WIDTH
OFF_A = OFF_BETA + GDN_HEADS
IN_PROJ = OFF_A + GDN_HEADS

LANES = 128
SUBLANES = 8
CHUNK = 128
COL_DT = 0
COL_BETA = SSM_HEADS
COL_G = SSM_HEADS + GDN_HEADS
SPLIT_LANES = 32
STREAMS = (("z", SSM_WIDTH), ("x", SSM_WIDTH), ("b", SSM_BC_WIDTH), ("c", SSM_BC_WIDTH),
           ("q", GDN_WIDTH), ("k", GDN_WIDTH), ("v", GDN_WIDTH), ("gate", GDN_WIDTH), ("small", LANES))
IN_PROJ_PAD = sum(w for _, w in STREAMS)
CONV_NAMES = ("x", "b", "c", "q", "k", "v")
CONV_TOTAL = sum(w for n, w in STREAMS if n in CONV_NAMES)
SSM_CONV_TOTAL = SSM_WIDTH + 2 * SSM_BC_WIDTH
PROJ_COLS = 512
CONV_ROWS = 128
NEG_BIG = -1e30
VMEM_LIMIT_BYTES = 56 * 1024 * 1024

TM_PROJ = 512
T_MIX = 256


NEG_LOG2_E = -1.4426950408889634


def _silu(x):
    return x / (1.0 + jnp.exp2(x * NEG_LOG2_E))


def _softplus(x):
    return jnp.maximum(x, 0.0) + jnp.log1p(jnp.exp(-jnp.abs(x)))


def _dot(a, b):
    return jnp.dot(a, b, preferred_element_type=F32)


def _dot_nt(a, b):
    return lax.dot_general(a, b, (((1,), (1,)), ((), ())), preferred_element_type=F32)


def _dot_tn(a, b):
    return lax.dot_general(a, b, (((0,), (0,)), ((), ())), preferred_element_type=F32)


def _compress3(y):
    hi = y.astype(BF16).astype(F32)
    r1 = y - hi
    mid = r1.astype(BF16).astype(F32)
    lo = r1 - mid
    return (hi + pltpu.roll(mid, SPLIT_LANES, 1) + pltpu.roll(lo, 2 * SPLIT_LANES, 1)).astype(BF16)


def _lane_iota(shape):
    return lax.broadcasted_iota(jnp.int32, shape, len(shape) - 1)


def _in_proj_chunks():
    conv, plain, w_off = [], [], 0
    for name, width in STREAMS:
        step = min(PROJ_COLS, width)
        for c0 in range(0, width, step):
            (conv if name in CONV_NAMES else plain).append((name, c0, w_off + c0, step))
        w_off += width
    chunks = []
    while conv or plain:
        if conv:
            chunks.append(conv.pop(0))
        if plain and len(plain) * 2 >= len(conv):
            chunks.append(plain.pop(0))
    return chunks


def _in_proj_body(h_ref, nw_ref, w_ref, cw_ref, cb_ref, *refs, tiles_per_seq):
    out_refs = dict(zip([n for n, _ in STREAMS], refs[:len(STREAMS)]))
    hn_ref = refs[len(STREAMS)]
    pad_refs = refs[len(STREAMS) + 1:]
    tm = h_ref.shape[0]

    @pl.when(pl.program_id(0) % tiles_per_seq == 0)
    def _():
        for pad_ref in pad_refs:
            pad_ref[0:SUBLANES, :] = jnp.zeros((SUBLANES, PROJ_COLS), F32)

    h = h_ref[...]
    ms = jnp.mean(h * h, axis=-1, keepdims=True)
    hn_ref[...] = (h * lax.rsqrt(ms + EPS) * nw_ref[...]).astype(BF16)

    def project(chunk):
        name, c0, w_off, width = chunk
        r = _dot(hn_ref[...], w_ref[:, w_off:w_off + width])
        if name in CONV_NAMES:
            pad_refs[(w_off - SSM_WIDTH) // PROJ_COLS][SUBLANES:SUBLANES + tm, :] = r
        else:
            out_refs[name][:, c0:c0 + width] = r.astype(out_refs[name].dtype)

    def finish(chunk):
        name, c0, w_off, width = chunk
        if name not in CONV_NAMES:
            return
        out = out_refs[name]
        cols = slice(w_off - SSM_WIDTH, w_off - SSM_WIDTH + width)
        pad_ref = pad_refs[(w_off - SSM_WIDTH) // PROJ_COLS]
        for r0 in range(0, tm, CONV_ROWS):
            blk = pad_ref[r0:r0 + CONV_ROWS + SUBLANES, :]
            acc = blk[SUBLANES:, :] * cw_ref[CONV_WIDTH - 1:CONV_WIDTH, cols]
            for j in range(CONV_WIDTH - 1):
                shifted = pltpu.roll(blk, CONV_WIDTH - 1 - j, 0)[SUBLANES:, :]
                acc = acc + shifted * cw_ref[j:j + 1, cols]
            if name in ("x", "b", "c"):
                acc = acc + cb_ref[:, cols]
            a = _silu(acc)
            if name in ("q", "k"):
                heads = []
                for hd in range(width // GDN_HEAD_DIM):
                    ah = a[:, hd * GDN_HEAD_DIM:(hd + 1) * GDN_HEAD_DIM]
                    ah = ah * lax.rsqrt(jnp.sum(ah * ah, axis=-1, keepdims=True) + EPS)
                    heads.append(ah * (GDN_HEAD_DIM ** -0.5) if name == "q" else ah)
                a = jnp.concatenate(heads, axis=1)
            out[r0:r0 + CONV_ROWS, c0:c0 + width] = a.astype(out.dtype)
        pad_ref[0:SUBLANES, :] = pad_ref[tm:tm + SUBLANES, :]

    chunks = _in_proj_chunks()
    project(chunks[0])
    for idx, chunk in enumerate(chunks):
        if idx + 1 < len(chunks):
            project(chunks[idx + 1])
        finish(chunk)


def _in_proj(h2d, norm_w, w_r, conv_w, conv_b, seqlen):
    n = h2d.shape[0]
    tm = min(TM_PROJ, seqlen)
    const = dict(pipeline_mode=pl.Buffered(1))
    out_shape = tuple(jax.ShapeDtypeStruct((n, w), F32 if name == "small" else BF16) for name, w in STREAMS)
    out_specs = tuple(pl.BlockSpec((tm, w), lambda i: (i, 0)) for _, w in STREAMS)
    return pl.pallas_call(
        functools.partial(_in_proj_body, tiles_per_seq=seqlen // tm),
        grid=(n // tm,),
        in_specs=[
            pl.BlockSpec((tm, D_MODEL), lambda i: (i, 0)),
            pl.BlockSpec((1, D_MODEL), lambda i: (0, 0)),
            pl.BlockSpec((D_MODEL, IN_PROJ_PAD), lambda i: (0, 0), **const),
            pl.BlockSpec((CONV_WIDTH, CONV_TOTAL), lambda i: (0, 0)),
            pl.BlockSpec((1, SSM_CONV_TOTAL), lambda i: (0, 0)),
        ],
        out_specs=out_specs,
        out_shape=out_shape,
        scratch_shapes=[pltpu.VMEM((tm, D_MODEL), BF16)]
        + [pltpu.VMEM((tm + SUBLANES, PROJ_COLS), F32) for _ in range(CONV_TOTAL // PROJ_COLS)],
        compiler_params=pltpu.CompilerParams(dimension_semantics=("arbitrary",),
                                             vmem_limit_bytes=VMEM_LIMIT_BYTES),
        name="in_proj",
    )(h2d, norm_w, w_r, conv_w, conv_b)


def _out_mlp_body(ymix_ref, x_ref, z_ref, gate_ref, h_ref, drow_ref, snorm_ref, gnorm_ref,
                  wout_ref, nw_ref, wup_ref, wdown_ref, fnw_ref, o_ref, *, final):
    h1 = h_ref[...]
    for g in range(SSM_GROUPS):
        cols = slice(g * SSM_GROUP_WIDTH, (g + 1) * SSM_GROUP_WIDTH)
        y = ymix_ref[:, cols].astype(F32) + drow_ref[:, cols] * x_ref[:, cols].astype(F32)
        y = y * _silu(z_ref[:, cols].astype(F32))
        ms = jnp.mean(y * y, axis=-1, keepdims=True)
        part = (y * lax.rsqrt(ms + EPS) * snorm_ref[:, cols]).astype(BF16)
        h1 = h1 + _dot(part, wout_ref[cols, :])
    for p in range(GDN_HEADS // 2):
        halves = []
        for hd in (2 * p, 2 * p + 1):
            hc = slice(hd * GDN_HEAD_DIM, (hd + 1) * GDN_HEAD_DIM)
            o = ymix_ref[:, SSM_WIDTH + hd * GDN_HEAD_DIM:SSM_WIDTH + (hd + 1) * GDN_HEAD_DIM].astype(F32)
            ms = jnp.mean(o * o, axis=-1, keepdims=True)
            o = o * lax.rsqrt(ms + EPS) * gnorm_ref[:, hc] * _silu(gate_ref[:, hc].astype(F32))
            halves.append(o.astype(BF16))
        rows = slice(SSM_WIDTH + 2 * p * GDN_HEAD_DIM, SSM_WIDTH + (2 * p + 2) * GDN_HEAD_DIM)
        h1 = h1 + _dot(jnp.concatenate(halves, axis=1), wout_ref[rows, :])
    ms = jnp.mean(h1 * h1, axis=-1, keepdims=True)
    hn = (h1 * lax.rsqrt(ms + EPS) * nw_ref[...]).astype(BF16)
    mlp = None
    for j in range(D_FF // D_MODEL):
        cols = slice(j * D_MODEL, (j + 1) * D_MODEL)
        up = _dot(hn, wup_ref[:, cols])
        act = jnp.square(jnp.maximum(up, 0.0)).astype(BF16)
        down = _dot(act, wdown_ref[cols, :])
        mlp = down if mlp is None else mlp + down
    acc = h1 + mlp
    if final:
        ms2 = jnp.mean(acc * acc, axis=-1, keepdims=True)
        acc = acc * lax.rsqrt(ms2 + EPS) * fnw_ref[...]
    o_ref[...] = acc


def _out_mlp(ymix, x_act, z, gate, h2d, drow, snorm, gnorm, w_out, norm_w, w_up, w_down, final_w, final):
    n = h2d.shape[0]
    tm = min(TM_PROJ, n)
    const = dict(pipeline_mode=pl.Buffered(1))

    def tok(width):
        return pl.BlockSpec((tm, width), lambda i: (i, 0))

    def row(width):
        return pl.BlockSpec((1, width), lambda i: (0, 0))

    return pl.pallas_call(
        functools.partial(_out_mlp_body, final=final),
        grid=(n // tm,),
        in_specs=[
            tok(MIX_WIDTH), tok(SSM_WIDTH), tok(SSM_WIDTH), tok(GDN_WIDTH), tok(D_MODEL),
            row(SSM_WIDTH), row(SSM_WIDTH), row(GDN_WIDTH),
            pl.BlockSpec((MIX_WIDTH, D_MODEL), lambda i: (0, 0), **const),
            row(D_MODEL),
            pl.BlockSpec((D_MODEL, D_FF), lambda i: (0, 0), **const),
            pl.BlockSpec((D_FF, D_MODEL), lambda i: (0, 0), **const),
            row(D_MODEL),
        ],
        out_specs=tok(D_MODEL),
        out_shape=jax.ShapeDtypeStruct((n, D_MODEL), F32),
        compiler_params=pltpu.CompilerParams(dimension_semantics=("arbitrary",),
                                             vmem_limit_bytes=VMEM_LIMIT_BYTES),
        name="out_mlp",
    )(ymix, x_act, z, gate, h2d, drow, snorm, gnorm, w_out, norm_w, w_up, w_down, final_w)


EXP_DT = 0
EXP_W = EXP_DT + SSM_WIDTH
EXP_EA = EXP_W + SSM_WIDTH
EXP_TOTAL = EXP_EA + SSM_WIDTH


def _mixer_body(x_ref, b_ref, c_ref, q_ref, k_ref, v_ref, small_ref, prow_ref, tri_ref, e_ssd_ref,
                out_ref,
                exp_ref, xdt_ref, xw_ref, ht_ref, s_ref, *, tile):
    nchunk = tile // CHUNK

    @pl.when(pl.program_id(1) == 0)
    def _():
        ht_ref[...] = jnp.zeros_like(ht_ref)
        s_ref[...] = jnp.zeros_like(s_ref)

    lane = _lane_iota((tile, LANES))
    sm = small_ref[...]
    bias = prow_ref[0:1, :]
    neg_a = -jnp.exp(prow_ref[1:2, :])
    sp = _softplus(sm + bias)
    is_dt = lane < COL_BETA
    is_beta = (lane >= COL_BETA) & (lane < COL_G)
    is_g = (lane >= COL_G) & (lane < SPLIT_LANES)
    steps = jnp.where(is_dt | is_g, sp * neg_a, 0.0)
    sc = _dot(tri_ref[...], _compress3(steps))
    cs = sc + pltpu.roll(sc, LANES - SPLIT_LANES, 1) + pltpu.roll(sc, LANES - 2 * SPLIT_LANES, 1)
    cs = jnp.where(lane < SPLIT_LANES, cs, 0.0)
    beta = jnp.where(is_beta, 1.0 / (1.0 + jnp.exp(-sm)), 0.0)
    dt = jnp.where(is_dt, sp, 0.0)
    to_end = jnp.concatenate(
        [jnp.exp(cs[(c + 1) * CHUNK - 1:(c + 1) * CHUNK, :] - cs[c * CHUNK:(c + 1) * CHUNK, :]) for c in range(nchunk)],
        axis=0)
    wfac = jnp.where(is_dt, dt * to_end, 0.0)
    ecs = jnp.exp(cs)
    ea = jnp.where(is_dt, ecs, 0.0)
    exp_ref[:, EXP_DT:EXP_DT + SSM_WIDTH] = _dot(_compress3(dt), e_ssd_ref[...])
    exp_ref[:, EXP_W:EXP_W + SSM_WIDTH] = _dot(_compress3(wfac), e_ssd_ref[...])
    exp_ref[:, EXP_EA:EXP_EA + SSM_WIDTH] = _dot(_compress3(ea), e_ssd_ref[...])

    xa = x_ref[...].astype(F32)
    xdt = xa * exp_ref[:, EXP_DT:EXP_DT + SSM_WIDTH]
    even_head = (_lane_iota((tile, SSM_WIDTH)) // SSM_HEAD_DIM) % 2 == 0
    xdt_ref[0] = jnp.where(even_head, xdt, 0.0).astype(BF16)
    xdt_ref[1] = jnp.where(even_head, 0.0, xdt).astype(BF16)
    xw_ref[...] = (xa * exp_ref[:, EXP_W:EXP_W + SSM_WIDTH]).astype(BF16)

    row_i = lax.broadcasted_iota(jnp.int32, (CHUNK, CHUNK), 0)
    col_i = lax.broadcasted_iota(jnp.int32, (CHUNK, CHUNK), 1)
    incl = row_i >= col_i
    strict = row_i > col_i
    eye = (row_i == col_i).astype(F32)
    merge_masks = [(row_i // 2 == col_i // 2) & strict]
    size = 2
    while size < CHUNK:
        merge_masks.append((row_i // (2 * size) == col_i // (2 * size))
                           & ((row_i // size) % 2 == 1) & ((col_i // size) % 2 == 0))
        size *= 2

    chunk_rows = [slice(c * CHUNK, (c + 1) * CHUNK) for c in range(nchunk)]
    chunk_last = [slice((c + 1) * CHUNK - 1, (c + 1) * CHUNK) for c in range(nchunk)]
    cs_t = [cs[rows, :].T for rows in chunk_rows]


    for c in range(nchunk):
        rows, last = chunk_rows[c], chunk_last[c]
        bgs = [b_ref[rows, g * SSM_STATE:(g + 1) * SSM_STATE] for g in range(SSM_GROUPS)]
        cgs = [c_ref[rows, g * SSM_STATE:(g + 1) * SSM_STATE] for g in range(SSM_GROUPS)]
        cbs = [_dot_nt(cgs[g], bgs[g]) for g in range(SSM_GROUPS)]
        y_pairs = []
        for p in range(SSM_HEADS // 2):
            ms = []
            for hh in (2 * p, 2 * p + 1):
                a_col = jnp.broadcast_to(cs[rows, hh:hh + 1], (CHUNK, CHUNK))
                a_row = cs_t[c][hh:hh + 1, :]
                decay = jnp.exp(jnp.where(incl, a_col - a_row, NEG_BIG))
                ms.append((cbs[hh // (SSM_HEADS // SSM_GROUPS)] * decay).astype(BF16))
            xs = jnp.concatenate([xdt_ref[i, rows, p * LANES:(p + 1) * LANES] for i in range(2)], axis=0)
            y_pairs.append(_dot(jnp.concatenate(ms, axis=1), xs))
        for g in range(SSM_GROUPS):
            gcols = slice(g * SSM_GROUP_WIDTH, (g + 1) * SSM_GROUP_WIDTH)
            ea_g = exp_ref[rows, EXP_EA + g * SSM_GROUP_WIDTH:EXP_EA + (g + 1) * SSM_GROUP_WIDTH]
            ea_last = exp_ref[last, EXP_EA + g * SSM_GROUP_WIDTH:EXP_EA + (g + 1) * SSM_GROUP_WIDTH]
            h_prev = ht_ref[g]
            y_off = _dot(cgs[g], h_prev.astype(BF16)) * ea_g
            out_ref[rows, gcols] = (jnp.concatenate(y_pairs[2 * g:2 * g + 2], axis=1) + y_off).astype(out_ref.dtype)
            ht_ref[g] = h_prev * ea_last + _dot_tn(bgs[g], xw_ref[rows, gcols])

    def head_col(tile_val, rows, lane_idx):
        return jnp.broadcast_to(tile_val[rows, lane_idx:lane_idx + 1], (CHUNK, GDN_HEAD_DIM))

    units = [(c, hd) for c in range(nchunk) for hd in range(GDN_HEADS)]
    a_mats, neg_as, qks, rhss, qdecs, kdecs, s_decays = {}, {}, {}, {}, {}, {}, {}
    for u in units:
        c, hd = u
        rows, last = chunk_rows[c], chunk_last[c]
        hcols = slice(hd * GDN_HEAD_DIM, (hd + 1) * GDN_HEAD_DIM)
        kh_b = k_ref[rows, hcols]
        qh = q_ref[rows, hcols].astype(F32)
        kh = kh_b.astype(F32)
        vh = v_ref[rows, hcols].astype(F32)
        be = head_col(beta, rows, COL_BETA + hd)
        gc = head_col(cs, rows, COL_G + hd)
        eg = head_col(ecs, rows, COL_G + hd)
        ek = head_col(to_end, rows, COL_G + hd)
        g_row = cs_t[c][COL_G + hd:COL_G + hd + 1, :]
        kb = kh * be
        gram = _dot_nt(jnp.concatenate([kb, qh], axis=0).astype(BF16), kh_b)
        decay = jnp.exp(jnp.where(incl, gc - g_row, NEG_BIG))
        a_mats[u] = jnp.where(strict, gram[0:CHUNK, :] * decay, 0.0)
        neg_as[u] = (-a_mats[u]).astype(BF16)
        qks[u] = (gram[CHUNK:2 * CHUNK, :] * decay).astype(BF16)
        rhss[u] = jnp.concatenate([vh * be, kb * eg], axis=1).astype(BF16)
        qdecs[u] = (qh * eg).astype(BF16)
        kdecs[u] = (kh * ek).astype(BF16)
        s_decays[u] = jnp.broadcast_to(ecs[last, COL_G + hd:COL_G + hd + 1], (1, GDN_HEAD_DIM))

    tinv = {u: eye - jnp.where(merge_masks[0], a_mats[u], 0.0) for u in units}
    for m in merge_masks[1:]:
        tinv_b = {u: tinv[u].astype(BF16) for u in units}
        inner = {u: _dot(neg_as[u], tinv_b[u]).astype(BF16) for u in units}
        tinv = {u: jnp.where(m, _dot(tinv_b[u], inner[u]), tinv[u]) for u in units}
    uws = {u: _dot(tinv[u].astype(BF16), rhss[u]) for u in units}

    states = [s_ref[hd] for hd in range(GDN_HEADS)]
    for c in range(nchunk):
        rows = chunk_rows[c]
        wqs = [_dot(jnp.concatenate([uws[(c, hd)][:, GDN_HEAD_DIM:].astype(BF16), qdecs[(c, hd)]], axis=0),
                    states[hd].astype(BF16)) for hd in range(GDN_HEADS)]
        v_news = [(uws[(c, hd)][:, 0:GDN_HEAD_DIM] - wqs[hd][0:CHUNK, :]).astype(BF16) for hd in range(GDN_HEADS)]
        for hd in range(GDN_HEADS):
            out_ref[rows, SSM_WIDTH + hd * GDN_HEAD_DIM:SSM_WIDTH + (hd + 1) * GDN_HEAD_DIM] = (
                wqs[hd][CHUNK:2 * CHUNK, :] + _dot(qks[(c, hd)], v_news[hd])).astype(out_ref.dtype)
        states = [states[hd] * s_decays[(c, hd)] + _dot_tn(kdecs[(c, hd)], v_news[hd]) for hd in range(GDN_HEADS)]
    for hd in range(GDN_HEADS):
        s_ref[hd] = states[hd]


def _expansion_matrix(first_row, heads, lanes_per_head):
    r = jnp.arange(LANES)[:, None]
    col_head = jnp.arange(heads * lanes_per_head)[None, :] // lanes_per_head
    src = (r % SPLIT_LANES) - first_row
    valid = r < 3 * SPLIT_LANES
    return ((src == col_head) & valid).astype(BF16)


def _mixer(streams, bsz, seqlen, prow):
    tile = min(T_MIX, seqlen)
    nt = seqlen // tile
    r = jnp.arange(tile)
    tri = ((r[:, None] >= r[None, :]) & (r[:, None] // CHUNK == r[None, :] // CHUNK)).astype(BF16)
    e_ssd = _expansion_matrix(COL_DT, SSM_HEADS, SSM_HEAD_DIM)

    def tok_spec(width):
        return pl.BlockSpec((None, tile, width), lambda b, t: (b, t, 0))

    def const_spec(arr):
        return pl.BlockSpec(arr.shape, lambda b, t: (0,) * arr.ndim)

    names = ("x", "b", "c", "q", "k", "v", "small")
    widths = dict(STREAMS)
    stream_arrays = [streams[name].reshape(bsz, seqlen, widths[name]) for name in names]
    consts = [prow, tri, e_ssd]
    return pl.pallas_call(
        functools.partial(_mixer_body, tile=tile),
        grid=(bsz, nt),
        in_specs=[tok_spec(widths[name]) for name in names] + [const_spec(a) for a in consts],
        out_specs=tok_spec(MIX_WIDTH),
        out_shape=jax.ShapeDtypeStruct((bsz, seqlen, MIX_WIDTH), BF16),
        scratch_shapes=[
            pltpu.VMEM((tile, EXP_TOTAL), F32),
            pltpu.VMEM((2, tile, SSM_WIDTH), BF16),
            pltpu.VMEM((tile, SSM_WIDTH), BF16),
            pltpu.VMEM((SSM_GROUPS, SSM_STATE, SSM_GROUP_WIDTH), F32),
            pltpu.VMEM((GDN_HEADS, GDN_HEAD_DIM, GDN_HEAD_DIM), F32),
        ],
        compiler_params=pltpu.CompilerParams(dimension_semantics=("arbitrary", "arbitrary"),
                                             vmem_limit_bytes=VMEM_LIMIT_BYTES),
        name="mixer",
    )(*stream_arrays, *consts)


def _rearranged_in_proj_weight(w_in):
    small = jnp.concatenate([w_in[:, OFF_DT:OFF_DT + SSM_HEADS], w_in[:, OFF_BETA:OFF_BETA + GDN_HEADS],
                             w_in[:, OFF_A:OFF_A + GDN_HEADS],
                             jnp.zeros((D_MODEL, LANES - SPLIT_LANES), w_in.dtype)], axis=1)
    parts = [w_in[:, OFF_Z:OFF_Z + SSM_WIDTH], w_in[:, OFF_X:OFF_X + SSM_WIDTH],
             w_in[:, OFF_B:OFF_B + SSM_BC_WIDTH], w_in[:, OFF_C:OFF_C + SSM_BC_WIDTH],
             w_in[:, OFF_Q:OFF_Q + GDN_WIDTH], w_in[:, OFF_K:OFF_K + GDN_WIDTH], w_in[:, OFF_V:OFF_V + GDN_WIDTH],
             w_in[:, OFF_GATE:OFF_GATE + GDN_WIDTH], small]
    return jnp.concatenate(parts, axis=1).astype(BF16)


def _scalar_rows(ssm_dt_bias, ssm_a_log, gdn_dt_bias, gdn_a_log):
    zeros8 = jnp.zeros((GDN_HEADS,), F32)
    tail = jnp.zeros((LANES - SPLIT_LANES,), F32)
    bias = jnp.concatenate([ssm_dt_bias.astype(F32), zeros8, gdn_dt_bias.astype(F32), tail])
    alog = jnp.concatenate([ssm_a_log.astype(F32), zeros8, gdn_a_log.astype(F32), tail])
    return jnp.concatenate([bias[None], alog[None], jnp.zeros((SUBLANES - 2, LANES), F32)], axis=0)


def kernel(x, attn_norm_w, w_in, ssm_conv_w, ssm_conv_b, ssm_dt_bias, ssm_a_log, ssm_d, ssm_norm_w, gdn_conv_w,
           gdn_dt_bias, gdn_a_log, gdn_norm_w, w_out, mlp_norm_w, w_up, w_down, final_norm_w):
    bsz, seqlen, _ = x.shape
    depth = w_in.shape[0]
    assert seqlen % CHUNK == 0 and seqlen % min(TM_PROJ, seqlen) == 0
    h = x.reshape(bsz * seqlen, D_MODEL).astype(F32)
    final_w = final_norm_w.reshape(1, D_MODEL).astype(F32)
    for layer in range(depth):
        conv_w = jnp.concatenate([ssm_conv_w[layer], gdn_conv_w[layer]], axis=1).astype(F32)
        outs = _in_proj(h, attn_norm_w[layer].reshape(1, D_MODEL).astype(F32), _rearranged_in_proj_weight(w_in[layer]),
                        conv_w, ssm_conv_b[layer].reshape(1, SSM_CONV_TOTAL).astype(F32), seqlen)
        streams = {name: o for (name, _), o in zip(STREAMS, outs)}
        ymix = _mixer(streams, bsz, seqlen,
                      _scalar_rows(ssm_dt_bias[layer], ssm_a_log[layer], gdn_dt_bias[layer], gdn_a_log[layer]))
        h = _out_mlp(ymix.reshape(bsz * seqlen, MIX_WIDTH), streams["x"], streams["z"], streams["gate"], h,
                     jnp.repeat(ssm_d[layer].astype(F32), SSM_HEAD_DIM).reshape(1, SSM_WIDTH),
                     ssm_norm_w[layer].reshape(1, SSM_WIDTH).astype(F32),
                     jnp.tile(gdn_norm_w[layer].astype(F32), GDN_HEADS).reshape(1, GDN_WIDTH),
                     w_out[layer].astype(BF16), mlp_norm_w[layer].reshape(1, D_MODEL).astype(F32),
                     w_up[layer].astype(BF16), w_down[layer].astype(BF16), final_w, final=(layer == depth - 1))
    return h.reshape(bsz, seqlen, D_MODEL).astype(x.dtype)
```

```python
import functools

import jax
import jax.numpy as jnp
from jax import lax
from jax.experimental import pallas as pl
from jax.experimental.pallas import tpu as pltpu

F32 = jnp.float32
BF16 = jnp.bfloat16

D_MODEL = 1024
SSM_HEADS = 16
SSM_HEAD_DIM = 64
SSM_WIDTH = SSM_HEADS * SSM_HEAD_DIM
SSM_GROUPS = 4
SSM_STATE = 128
SSM_BC_WIDTH = SSM_GROUPS * SSM_STATE
SSM_GROUP_WIDTH = SSM_WIDTH // SSM_GROUPS
GDN_HEADS = 8
GDN_HEAD_DIM = 128
GDN_WIDTH = GDN_HEADS * GDN_HEAD_DIM
CONV_WIDTH = 4
MIX_WIDTH = SSM_WIDTH + GDN_WIDTH
D_FF = 4 * D_MODEL
EPS = 1e-6

OFF_Z = 0
OFF_X = SSM_WIDTH
OFF_B = OFF_X + SSM_WIDTH
OFF_C = OFF_B + SSM_BC_WIDTH
OFF_DT = OFF_C + SSM_BC_WIDTH
OFF_Q = OFF_DT + SSM_HEADS
OFF_K = OFF_Q + GDN_WIDTH
OFF_V = OFF_K + GDN_WIDTH
OFF_GATE = OFF_V + GDN_WIDTH
OFF_BETA = OFF_GATE + GDN_WIDTH
OFF_A = OFF_BETA + GDN_HEADS
IN_PROJ = OFF_A + GDN_HEADS

LANES = 128
SUBLANES = 8
CHUNK = 128
COL_DT = 0
COL_BETA = SSM_HEADS
COL_G = SSM_HEADS + GDN_HEADS
SPLIT_LANES = 32
STREAMS = (("z", SSM_WIDTH), ("x", SSM_WIDTH), ("b", SSM_BC_WIDTH), ("c", SSM_BC_WIDTH),
           ("q", GDN_WIDTH), ("k", GDN_WIDTH), ("v", GDN_WIDTH), ("gate", GDN_WIDTH), ("small", LANES))
IN_PROJ_PAD = sum(w for _, w in STREAMS)
CONV_NAMES = ("x", "b", "c", "q", "k", "v")
CONV_TOTAL = sum(w for n, w in STREAMS if n in CONV_NAMES)
SSM_CONV_TOTAL = SSM_WIDTH + 2 * SSM_BC_WIDTH
PROJ_COLS = 512
CONV_ROWS = 128
NEG_BIG = -1e30
VMEM_LIMIT_BYTES = 56 * 1024 * 1024

TM_PROJ = 512
T_MIX = 512


NEG_LOG2_E = -1.4426950408889634


def _silu(x):
    return x / (1.0 + jnp.exp2(x * NEG_LOG2_E))


def _softplus(x):
    return jnp.maximum(x, 0.0) + jnp.log1p(jnp.exp(-jnp.abs(x)))


def _dot(a, b):
    return jnp.dot(a, b, preferred_element_type=F32)


def _dot_nt(a, b):
    return lax.dot_general(a, b, (((1,), (1,)), ((), ())), preferred_element_type=F32)


def _dot_tn(a, b):
    return lax.dot_general(a, b, (((0,), (0,)), ((), ())), preferred_element_type=F32)


def _compress3(y):
    hi = y.astype(BF16).astype(F32)
    r1 = y - hi
    mid = r1.astype(BF16).astype(F32)
    lo = r1 - mid
    return (hi + pltpu.roll(mid, SPLIT_LANES, 1) + pltpu.roll(lo, 2 * SPLIT_LANES, 1)).astype(BF16)


def _lane_iota(shape):
    return lax.broadcasted_iota(jnp.int32, shape, len(shape) - 1)


def _in_proj_chunks():
    conv, plain, w_off = [], [], 0
    for name, width in STREAMS:
        step = min(PROJ_COLS, width)
        for c0 in range(0, width, step):
            (conv if name in CONV_NAMES else plain).append((name, c0, w_off + c0, step))
        w_off += width
    chunks = []
    while conv or plain:
        if conv:
            chunks.append(conv.pop(0))
        if plain and len(plain) * 2 >= len(conv):
            chunks.append(plain.pop(0))
    return chunks


def _in_proj_body(h_ref, nw_ref, cw_ref, cb_ref, *refs, tiles_per_seq):
    ns = len(STREAMS)
    w_refs = dict(zip([n for n, _ in STREAMS], refs[:ns]))
    out_refs = dict(zip([n for n, _ in STREAMS], refs[ns:2 * ns]))
    hn_ref = refs[2 * ns]
    pad_refs = refs[2 * ns + 1:]
    tm = h_ref.shape[0]

    @pl.when(pl.program_id(0) % tiles_per_seq == 0)
    def _():
        for pad_ref in pad_refs:
            pad_ref[0:SUBLANES, :] = jnp.zeros((SUBLANES, PROJ_COLS), F32)

    h = h_ref[...]
    ms = jnp.mean(h * h, axis=-1, keepdims=True)
    hn_ref[...] = (h * lax.rsqrt(ms + EPS) * nw_ref[...]).astype(BF16)

    def project(chunk):
        name, c0, w_off, width = chunk
        r = _dot(hn_ref[...], w_refs[name][:, c0:c0 + width])
        if name in CONV_NAMES:
            pad_refs[(w_off - SSM_WIDTH) // PROJ_COLS][SUBLANES:SUBLANES + tm, :] = r
        else:
            out_refs[name][:, c0:c0 + width] = r.astype(out_refs[name].dtype)

    def finish(chunk):
        name, c0, w_off, width = chunk
        if name not in CONV_NAMES:
            return
        out = out_refs[name]
        cols = slice(w_off - SSM_WIDTH, w_off - SSM_WIDTH + width)
        pad_ref = pad_refs[(w_off - SSM_WIDTH) // PROJ_COLS]
        for r0 in range(0, tm, CONV_ROWS):
            for l0 in range(0, width, LANES):
                wc = slice(cols.start + l0, cols.start + l0 + LANES)
                blk = pad_ref[r0:r0 + CONV_ROWS + SUBLANES, l0:l0 + LANES]
                acc = blk[SUBLANES:, :] * cw_ref[CONV_WIDTH - 1:CONV_WIDTH, wc]
                for j in range(CONV_WIDTH - 1):
                    shifted = pltpu.roll(blk, CONV_WIDTH - 1 - j, 0)[SUBLANES:, :]
                    acc = acc + shifted * cw_ref[j:j + 1, wc]
                if name in ("x", "b", "c"):
                    acc = acc + cb_ref[:, wc]
                a = _silu(acc)
                if name in ("q", "k"):
                    a = a * lax.rsqrt(jnp.sum(a * a, axis=-1, keepdims=True) + EPS)
                    if name == "q":
                        a = a * (GDN_HEAD_DIM ** -0.5)
                out[r0:r0 + CONV_ROWS, c0 + l0:c0 + l0 + LANES] = a.astype(out.dtype)
        pad_ref[0:SUBLANES, :] = pad_ref[tm:tm + SUBLANES, :]

    chunks = _in_proj_chunks()
    project(chunks[0])
    for idx, chunk in enumerate(chunks):
        if idx + 1 < len(chunks):
            project(chunks[idx + 1])
        finish(chunk)


def _in_proj(h2d, norm_w, w_parts, conv_w, conv_b, seqlen):
    n = h2d.shape[0]
    tm = min(TM_PROJ, seqlen)
    const = dict(pipeline_mode=pl.Buffered(1))
    out_shape = tuple(jax.ShapeDtypeStruct((n, w), F32 if name == "small" else BF16) for name, w in STREAMS)
    out_specs = tuple(pl.BlockSpec((tm, w), lambda i: (i, 0)) for _, w in STREAMS)
    return pl.pallas_call(
        functools.partial(_in_proj_body, tiles_per_seq=seqlen // tm),
        grid=(n // tm,),
        in_specs=[
            pl.BlockSpec((tm, D_MODEL), lambda i: (i, 0)),
            pl.BlockSpec((1, D_MODEL), lambda i: (0, 0)),
            pl.BlockSpec((CONV_WIDTH, CONV_TOTAL), lambda i: (0, 0)),
            pl.BlockSpec((1, SSM_CONV_TOTAL), lambda i: (0, 0)),
        ] + [pl.BlockSpec((D_MODEL, w), lambda i: (0, 0), **const) for _, w in STREAMS],
        out_specs=out_specs,
        out_shape=out_shape,
        scratch_shapes=[pltpu.VMEM((tm, D_MODEL), BF16)]
        + [pltpu.VMEM((tm + SUBLANES, PROJ_COLS), F32) for _ in range(CONV_TOTAL // PROJ_COLS)],
        compiler_params=pltpu.CompilerParams(dimension_semantics=("arbitrary",),
                                             vmem_limit_bytes=VMEM_LIMIT_BYTES),
        name="in_proj",
    )(h2d, norm_w, conv_w, conv_b, *w_parts)


def _out_mlp_body(ymix_ref, x_ref, z_ref, gate_ref, h_ref, drow_ref, snorm_ref, gnorm_ref,
                  wout_ref, nw_ref, wup_ref, wdown_ref, fnw_ref, o_ref, *, final):
    h1 = h_ref[...]
    for g in range(SSM_GROUPS):
        cols = slice(g * SSM_GROUP_WIDTH, (g + 1) * SSM_GROUP_WIDTH)
        y = ymix_ref[:, cols].astype(F32) + drow_ref[:, cols] * x_ref[:, cols].astype(F32)
        y = y * _silu(z_ref[:, cols].astype(F32))
        ms = jnp.mean(y * y, axis=-1, keepdims=True)
        part = (y * lax.rsqrt(ms + EPS) * snorm_ref[:, cols]).astype(BF16)
        h1 = h1 + _dot(part, wout_ref[cols, :])
    for p in range(GDN_HEADS // 2):
        halves = []
        for hd in (2 * p, 2 * p + 1):
            hc = slice(hd * GDN_HEAD_DIM, (hd + 1) * GDN_HEAD_DIM)
            o = ymix_ref[:, SSM_WIDTH + hd * GDN_HEAD_DIM:SSM_WIDTH + (hd + 1) * GDN_HEAD_DIM].astype(F32)
            ms = jnp.mean(o * o, axis=-1, keepdims=True)
            o = o * lax.rsqrt(ms + EPS) * gnorm_ref[:, hc] * _silu(gate_ref[:, hc].astype(F32))
            halves.append(o.astype(BF16))
        rows = slice(SSM_WIDTH + 2 * p * GDN_HEAD_DIM, SSM_WIDTH + (2 * p + 2) * GDN_HEAD_DIM)
        h1 = h1 + _dot(jnp.concatenate(halves, axis=1), wout_ref[rows, :])
    ms = jnp.mean(h1 * h1, axis=-1, keepdims=True)
    hn = (h1 * lax.rsqrt(ms + EPS) * nw_ref[...]).astype(BF16)
    mlp = None
    for j in range(D_FF // D_MODEL):
        cols = slice(j * D_MODEL, (j + 1) * D_MODEL)
        up = _dot(hn, wup_ref[:, cols])
        act = jnp.square(jnp.maximum(up, 0.0)).astype(BF16)
        down = _dot(act, wdown_ref[cols, :])
        mlp = down if mlp is None else mlp + down
    acc = h1 + mlp
    if final:
        ms2 = jnp.mean(acc * acc, axis=-1, keepdims=True)
        acc = acc * lax.rsqrt(ms2 + EPS) * fnw_ref[...]
    o_ref[...] = acc


def _out_mlp(ymix, x_act, z, gate, h2d, drow, snorm, gnorm, w_out, norm_w, w_up, w_down, final_w, final):
    n = h2d.shape[0]
    tm = min(TM_PROJ, n)
    const = dict(pipeline_mode=pl.Buffered(1))

    def tok(width):
        return pl.BlockSpec((tm, width), lambda i: (i, 0))

    def row(width):
        return pl.BlockSpec((1, width), lambda i: (0, 0))

    return pl.pallas_call(
        functools.partial(_out_mlp_body, final=final),
        grid=(n // tm,),
        in_specs=[
            tok(MIX_WIDTH), tok(SSM_WIDTH), tok(SSM_WIDTH), tok(GDN_WIDTH), tok(D_MODEL),
            row(SSM_WIDTH), row(SSM_WIDTH), row(GDN_WIDTH),
            pl.BlockSpec((MIX_WIDTH, D_MODEL), lambda i: (0, 0), **const),
            row(D_MODEL),
            pl.BlockSpec((D_MODEL, D_FF), lambda i: (0, 0), **const),
            pl.BlockSpec((D_FF, D_MODEL), lambda i: (0, 0), **const),
            row(D_MODEL),
        ],
        out_specs=tok(D_MODEL),
        out_shape=jax.ShapeDtypeStruct((n, D_MODEL), F32),
        compiler_params=pltpu.CompilerParams(dimension_semantics=("arbitrary",),
                                             vmem_limit_bytes=VMEM_LIMIT_BYTES),
        name="out_mlp",
    )(ymix, x_act, z, gate, h2d, drow, snorm, gnorm, w_out, norm_w, w_up, w_down, final_w)


EXP_DT = 0
EXP_W = EXP_DT + SSM_WIDTH
EXP_EA = EXP_W + SSM_WIDTH
EXP_TOTAL = EXP_EA + SSM_WIDTH


def _mixer_body(x_ref, b_ref, c_ref, q_ref, k_ref, v_ref, small_ref, prow_ref, tri_ref, e_ssd_ref,
                out_ref,
                exp_ref, xdt_ref, xw_ref, ht_ref, s_ref, *, tile):
    nchunk = tile // CHUNK

    @pl.when(pl.program_id(1) == 0)
    def _():
        ht_ref[...] = jnp.zeros_like(ht_ref)
        s_ref[...] = jnp.zeros_like(s_ref)

    lane = _lane_iota((tile, LANES))
    sm = small_ref[...]
    bias = prow_ref[0:1, :]
    neg_a = -jnp.exp(prow_ref[1:2, :])
    sp = _softplus(sm + bias)
    is_dt = lane < COL_BETA
    is_beta = (lane >= COL_BETA) & (lane < COL_G)
    is_g = (lane >= COL_G) & (lane < SPLIT_LANES)
    steps = jnp.where(is_dt | is_g, sp * neg_a, 0.0)
    sc = _dot(tri_ref[...], _compress3(steps))
    cs = sc + pltpu.roll(sc, LANES - SPLIT_LANES, 1) + pltpu.roll(sc, LANES - 2 * SPLIT_LANES, 1)
    cs = jnp.where(lane < SPLIT_LANES, cs, 0.0)
    beta = jnp.where(is_beta, 1.0 / (1.0 + jnp.exp(-sm)), 0.0)
    dt = jnp.where(is_dt, sp, 0.0)
    to_end = jnp.concatenate(
        [jnp.exp(cs[(c + 1) * CHUNK - 1:(c + 1) * CHUNK, :] - cs[c * CHUNK:(c + 1) * CHUNK, :]) for c in range(nchunk)],
        axis=0)
    wfac = jnp.where(is_dt, dt * to_end, 0.0)
    ecs = jnp.exp(cs)
    ea = jnp.where(is_dt, ecs, 0.0)
    exp_ref[:, EXP_DT:EXP_DT + SSM_WIDTH] = _dot(_compress3(dt), e_ssd_ref[...])
    exp_ref[:, EXP_W:EXP_W + SSM_WIDTH] = _dot(_compress3(wfac), e_ssd_ref[...])
    exp_ref[:, EXP_EA:EXP_EA + SSM_WIDTH] = _dot(_compress3(ea), e_ssd_ref[...])

    xa = x_ref[...].astype(F32)
    xdt = xa * exp_ref[:, EXP_DT:EXP_DT + SSM_WIDTH]
    even_head = (_lane_iota((tile, SSM_WIDTH)) // SSM_HEAD_DIM) % 2 == 0
    xdt_ref[0] = jnp.where(even_head, xdt, 0.0).astype(BF16)
    xdt_ref[1] = jnp.where(even_head, 0.0, xdt).astype(BF16)
    xw_ref[...] = (xa * exp_ref[:, EXP_W:EXP_W + SSM_WIDTH]).astype(BF16)

    row_i = lax.broadcasted_iota(jnp.int32, (CHUNK, CHUNK), 0)
    col_i = lax.broadcasted_iota(jnp.int32, (CHUNK, CHUNK), 1)
    incl = row_i >= col_i
    strict = row_i > col_i
    eye = (row_i == col_i).astype(F32)
    merge_masks = [(row_i // 2 == col_i // 2) & strict]
    size = 2
    while size < CHUNK:
        merge_masks.append((row_i // (2 * size) == col_i // (2 * size))
                           & ((row_i // size) % 2 == 1) & ((col_i // size) % 2 == 0))
        size *= 2

    chunk_rows = [slice(c * CHUNK, (c + 1) * CHUNK) for c in range(nchunk)]
    chunk_last = [slice((c + 1) * CHUNK - 1, (c + 1) * CHUNK) for c in range(nchunk)]
    cs_t = [cs[rows, :].T for rows in chunk_rows]


    for c in range(nchunk):
        rows, last = chunk_rows[c], chunk_last[c]
        bgs = [b_ref[rows, g * SSM_STATE:(g + 1) * SSM_STATE] for g in range(SSM_GROUPS)]
        cgs = [c_ref[rows, g * SSM_STATE:(g + 1) * SSM_STATE] for g in range(SSM_GROUPS)]
        cbs = [_dot_nt(cgs[g], bgs[g]) for g in range(SSM_GROUPS)]
        y_pairs = []
        for p in range(SSM_HEADS // 2):
            ms = []
            for hh in (2 * p, 2 * p + 1):
                a_col = jnp.broadcast_to(cs[rows, hh:hh + 1], (CHUNK, CHUNK))
                a_row = cs_t[c][hh:hh + 1, :]
                decay = jnp.exp(jnp.where(incl, a_col - a_row, NEG_BIG))
                ms.append((cbs[hh // (SSM_HEADS // SSM_GROUPS)] * decay).astype(BF16))
            xs = jnp.concatenate([xdt_ref[i, rows, p * LANES:(p + 1) * LANES] for i in range(2)], axis=0)
            y_pairs.append(_dot(jnp.concatenate(ms, axis=1), xs))
        for g in range(SSM_GROUPS):
            gcols = slice(g * SSM_GROUP_WIDTH, (g + 1) * SSM_GROUP_WIDTH)
            ea_g = exp_ref[rows, EXP_EA + g * SSM_GROUP_WIDTH:EXP_EA + (g + 1) * SSM_GROUP_WIDTH]
            ea_last = exp_ref[last, EXP_EA + g * SSM_GROUP_WIDTH:EXP_EA + (g + 1) * SSM_GROUP_WIDTH]
            h_prev = ht_ref[g]
            y_off = _dot(cgs[g], h_prev.astype(BF16)) * ea_g
            out_ref[rows, gcols] = (jnp.concatenate(y_pairs[2 * g:2 * g + 2], axis=1) + y_off).astype(out_ref.dtype)
            ht_ref[g] = h_prev * ea_last + _dot_tn(bgs[g], xw_ref[rows, gcols])

    def head_col(tile_val, rows, lane_idx):
        return jnp.broadcast_to(tile_val[rows, lane_idx:lane_idx + 1], (CHUNK, GDN_HEAD_DIM))

    units = [(c, hd) for c in range(nchunk) for hd in range(GDN_HEADS)]
    a_mats, neg_as, qks, rhss, qdecs, kdecs, s_decays = {}, {}, {}, {}, {}, {}, {}
    for u in units:
        c, hd = u
        rows, last = chunk_rows[c], chunk_last[c]
        hcols = slice(hd * GDN_HEAD_DIM, (hd + 1) * GDN_HEAD_DIM)
        kh_b = k_ref[rows, hcols]
        qh = q_ref[rows, hcols].astype(F32)
        kh = kh_b.astype(F32)
        vh = v_ref[rows, hcols].astype(F32)
        be = head_col(beta, rows, COL_BETA + hd)
        gc = head_col(cs, rows, COL_G + hd)
        eg = head_col(ecs, rows, COL_G + hd)
        ek = head_col(to_end, rows, COL_G + hd)
        g_row = cs_t[c][COL_G + hd:COL_G + hd + 1, :]
        kb = kh * be
        gram = _dot_nt(jnp.concatenate([kb, qh], axis=0).astype(BF16), kh_b)
        decay = jnp.exp(jnp.where(incl, gc - g_row, NEG_BIG))
        a_mats[u] = jnp.where(strict, gram[0:CHUNK, :] * decay, 0.0)
        neg_as[u] = (-a_mats[u]).astype(BF16)
        qks[u] = (gram[CHUNK:2 * CHUNK, :] * decay).astype(BF16)
        rhss[u] = jnp.concatenate([vh * be, kb * eg], axis=1).astype(BF16)
        qdecs[u] = (qh * eg).astype(BF16)
        kdecs[u] = (kh * ek).astype(BF16)
        s_decays[u] = jnp.broadcast_to(ecs[last, COL_G + hd:COL_G + hd + 1], (1, GDN_HEAD_DIM))

    tinv = {u: eye - jnp.where(merge_masks[0], a_mats[u], 0.0) for u in units}
    for m in merge_masks[1:]:
        tinv_b = {u: tinv[u].astype(BF16) for u in units}
        inner = {u: _dot(neg_as[u], tinv_b[u]).astype(BF16) for u in units}
        tinv = {u: jnp.where(m, _dot(tinv_b[u], inner[u]), tinv[u]) for u in units}
    uws = {u: _dot(tinv[u].astype(BF16), rhss[u]) for u in units}

    states = [s_ref[hd] for hd in range(GDN_HEADS)]
    for c in range(nchunk):
        rows = chunk_rows[c]
        wqs = [_dot(jnp.concatenate([uws[(c, hd)][:, GDN_HEAD_DIM:].astype(BF16), qdecs[(c, hd)]], axis=0),
                    states[hd].astype(BF16)) for hd in range(GDN_HEADS)]
        v_news = [(uws[(c, hd)][:, 0:GDN_HEAD_DIM] - wqs[hd][0:CHUNK, :]).astype(BF16) for hd in range(GDN_HEADS)]
        for hd in range(GDN_HEADS):
            out_ref[rows, SSM_WIDTH + hd * GDN_HEAD_DIM:SSM_WIDTH + (hd + 1) * GDN_HEAD_DIM] = (
                wqs[hd][CHUNK:2 * CHUNK, :] + _dot(qks[(c, hd)], v_news[hd])).astype(out_ref.dtype)
        states = [states[hd] * s_decays[(c, hd)] + _dot_tn(kdecs[(c, hd)], v_news[hd]) for hd in range(GDN_HEADS)]
    for hd in range(GDN_HEADS):
        s_ref[hd] = states[hd]


def _expansion_matrix(first_row, heads, lanes_per_head):
    r = jnp.arange(LANES)[:, None]
    col_head = jnp.arange(heads * lanes_per_head)[None, :] // lanes_per_head
    src = (r % SPLIT_LANES) - first_row
    valid = r < 3 * SPLIT_LANES
    return ((src == col_head) & valid).astype(BF16)


def _mixer(streams, bsz, seqlen, prow):
    tile = min(T_MIX, seqlen)
    nt = seqlen // tile
    r = jnp.arange(tile)
    tri = ((r[:, None] >= r[None, :]) & (r[:, None] // CHUNK == r[None, :] // CHUNK)).astype(BF16)
    e_ssd = _expansion_matrix(COL_DT, SSM_HEADS, SSM_HEAD_DIM)

    def tok_spec(width):
        return pl.BlockSpec((None, tile, width), lambda b, t: (b, t, 0))

    def const_spec(arr):
        return pl.BlockSpec(arr.shape, lambda b, t: (0,) * arr.ndim)

    names = ("x", "b", "c", "q", "k", "v", "small")
    widths = dict(STREAMS)
    stream_arrays = [streams[name].reshape(bsz, seqlen, widths[name]) for name in names]
    consts = [prow, tri, e_ssd]
    return pl.pallas_call(
        functools.partial(_mixer_body, tile=tile),
        grid=(bsz, nt),
        in_specs=[tok_spec(widths[name]) for name in names] + [const_spec(a) for a in consts],
        out_specs=tok_spec(MIX_WIDTH),
        out_shape=jax.ShapeDtypeStruct((bsz, seqlen, MIX_WIDTH), BF16),
        scratch_shapes=[
            pltpu.VMEM((tile, EXP_TOTAL), F32),
            pltpu.VMEM((2, tile, SSM_WIDTH), BF16),
            pltpu.VMEM((tile, SSM_WIDTH), BF16),
            pltpu.VMEM((SSM_GROUPS, SSM_STATE, SSM_GROUP_WIDTH), F32),
            pltpu.VMEM((GDN_HEADS, GDN_HEAD_DIM, GDN_HEAD_DIM), F32),
        ],
        compiler_params=pltpu.CompilerParams(dimension_semantics=("arbitrary", "arbitrary"),
                                             vmem_limit_bytes=VMEM_LIMIT_BYTES),
        name="mixer",
    )(*stream_arrays, *consts)


def _in_proj_weight_parts(w_in):
    small = jnp.concatenate([w_in[:, OFF_DT:OFF_DT + SSM_HEADS], w_in[:, OFF_BETA:OFF_BETA + GDN_HEADS],
                             w_in[:, OFF_A:OFF_A + GDN_HEADS],
                             jnp.zeros((D_MODEL, LANES - SPLIT_LANES), w_in.dtype)], axis=1)
    parts = [w_in[:, OFF_Z:OFF_Z + SSM_WIDTH], w_in[:, OFF_X:OFF_X + SSM_WIDTH],
             w_in[:, OFF_B:OFF_B + SSM_BC_WIDTH], w_in[:, OFF_C:OFF_C + SSM_BC_WIDTH],
             w_in[:, OFF_Q:OFF_Q + GDN_WIDTH], w_in[:, OFF_K:OFF_K + GDN_WIDTH], w_in[:, OFF_V:OFF_V + GDN_WIDTH],
             w_in[:, OFF_GATE:OFF_GATE + GDN_WIDTH], small]
    return tuple(p.astype(BF16) for p in parts)


def _scalar_rows(ssm_dt_bias, ssm_a_log, gdn_dt_bias, gdn_a_log):
    zeros8 = jnp.zeros((GDN_HEADS,), F32)
    tail = jnp.zeros((LANES - SPLIT_LANES,), F32)
    bias = jnp.concatenate([ssm_dt_bias.astype(F32), zeros8, gdn_dt_bias.astype(F32), tail])
    alog = jnp.concatenate([ssm_a_log.astype(F32), zeros8, gdn_a_log.astype(F32), tail])
    return jnp.concatenate([bias[None], alog[None], jnp.zeros((SUBLANES - 2, LANES), F32)], axis=0)


def kernel(x, attn_norm_w, w_in, ssm_conv_w, ssm_conv_b, ssm_dt_bias, ssm_a_log, ssm_d, ssm_norm_w, gdn_conv_w,
           gdn_dt_bias, gdn_a_log, gdn_norm_w, w_out, mlp_norm_w, w_up, w_down, final_norm_w):
    bsz, seqlen, _ = x.shape
    depth = w_in.shape[0]
    assert seqlen % CHUNK == 0 and seqlen % min(TM_PROJ, seqlen) == 0
    h = x.reshape(bsz * seqlen, D_MODEL).astype(F32)
    final_w = final_norm_w.reshape(1, D_MODEL).astype(F32)
    for layer in range(depth):
        conv_w = jnp.concatenate([ssm_conv_w[layer], gdn_conv_w[layer]], axis=1).astype(F32)
        outs = _in_proj(h, attn_norm_w[layer].reshape(1, D_MODEL).astype(F32), _in_proj_weight_parts(w_in[layer]),
                        conv_w, ssm_conv_b[layer].reshape(1, SSM_CONV_TOTAL).astype(F32), seqlen)
        streams = {name: o for (name, _), o in zip(STREAMS, outs)}
        ymix = _mixer(streams, bsz, seqlen,
                      _scalar_rows(ssm_dt_bias[layer], ssm_a_log[layer], gdn_dt_bias[layer], gdn_a_log[layer]))
        h = _out_mlp(ymix.reshape(bsz * seqlen, MIX_WIDTH), streams["x"], streams["z"], streams["gate"], h,
                     jnp.repeat(ssm_d[layer].astype(F32), SSM_HEAD_DIM).reshape(1, SSM_WIDTH),
                     ssm_norm_w[layer].reshape(1, SSM_WIDTH).astype(F32),
                     jnp.tile(gdn_norm_w[layer].astype(F32), GDN_HEADS).reshape(1, GDN_WIDTH),
                     w_out[layer].astype(BF16), mlp_norm_w[layer].reshape(1, D_MODEL).astype(F32),
                     w_up[layer].astype(BF16), w_down[layer].astype(BF16), final_w, final=(layer == depth - 1))
    return h.reshape(bsz, seqlen, D_MODEL).astype(x.dtype)
```

```python
import functools

import jax
import jax.numpy as jnp
from jax import lax
from jax.experimental import pallas as pl
from jax.experimental.pallas import tpu as pltpu

F32 = jnp.float32
BF16 = jnp.bfloat16

D_MODEL = 1024
SSM_HEADS = 16
SSM_HEAD_DIM = 64
SSM_WIDTH = SSM_HEADS * SSM_HEAD_DIM
SSM_GROUPS = 4
SSM_STATE = 128
SSM_BC_WIDTH = SSM_GROUPS * SSM_STATE
SSM_GROUP_WIDTH = SSM_WIDTH // SSM_GROUPS
GDN_HEADS = 8
GDN_HEAD_DIM = 128
GDN_WIDTH = GDN_HEADS * GDN_HEAD_DIM
CONV_WIDTH = 4
MIX_WIDTH = SSM_WIDTH + GDN_WIDTH
D_FF = 4 * D_MODEL
EPS = 1e-6

OFF_Z = 0
OFF_X = SSM_WIDTH
OFF_B = OFF_X + SSM_WIDTH
OFF_C = OFF_B + SSM_BC_WIDTH
OFF_DT = OFF_C + SSM_BC_WIDTH
OFF_Q = OFF_DT + SSM_HEADS
OFF_K = OFF_Q + GDN_WIDTH
OFF_V = OFF_K + GDN_WIDTH
OFF_GATE = OFF_V + GDN_WIDTH
OFF_BETA = OFF_GATE + GDN_WIDTH
OFF_A = OFF_BETA + GDN_HEADS
IN_PROJ = OFF_A + GDN_HEADS

LANES = 128
SUBLANES = 8
CHUNK = 128
COL_DT = 0
COL_BETA = SSM_HEADS
COL_G = SSM_HEADS + GDN_HEADS
SPLIT_LANES = 32
STREAMS = (("z", SSM_WIDTH), ("x", SSM_WIDTH), ("b", SSM_BC_WIDTH), ("c", SSM_BC_WIDTH),
           ("q", GDN_WIDTH), ("k", GDN_WIDTH), ("v", GDN_WIDTH), ("gate", GDN_WIDTH), ("small", LANES))
IN_PROJ_PAD = sum(w for _, w in STREAMS)
CONV_NAMES = ("x", "b", "c", "q", "k", "v")
CONV_TOTAL = sum(w for n, w in STREAMS if n in CONV_NAMES)
SSM_CONV_TOTAL = SSM_WIDTH + 2 * SSM_BC_WIDTH
PROJ_COLS = 512
CONV_ROWS = 128
NEG_BIG = -1e30
VMEM_LIMIT_BYTES = 56 * 1024 * 1024

TM_PROJ = 512
T_MIX = 512


NEG_LOG2_E = -1.4426950408889634


def _silu(x):
    return x / (1.0 + jnp.exp2(x * NEG_LOG2_E))


def _softplus(x):
    return jnp.maximum(x, 0.0) + jnp.log1p(jnp.exp(-jnp.abs(x)))


def _dot(a, b):
    return jnp.dot(a, b, preferred_element_type=F32)


def _dot_nt(a, b):
    return lax.dot_general(a, b, (((1,), (1,)), ((), ())), preferred_element_type=F32)


def _dot_tn(a, b):
    return lax.dot_general(a, b, (((0,), (0,)), ((), ())), preferred_element_type=F32)


def _compress3(y):
    hi = y.astype(BF16).astype(F32)
    r1 = y - hi
    mid = r1.astype(BF16).astype(F32)
    lo = r1 - mid
    return (hi + pltpu.roll(mid, SPLIT_LANES, 1) + pltpu.roll(lo, 2 * SPLIT_LANES, 1)).astype(BF16)


def _lane_iota(shape):
    return lax.broadcasted_iota(jnp.int32, shape, len(shape) - 1)


def _in_proj_chunks():
    conv, plain, w_off = [], [], 0
    for name, width in STREAMS:
        step = min(PROJ_COLS, width)
        for c0 in range(0, width, step):
            (conv if name in CONV_NAMES else plain).append((name, c0, w_off + c0, step))
        w_off += width
    chunks = []
    while conv or plain:
        if conv:
            chunks.append(conv.pop(0))
        if plain and len(plain) * 2 >= len(conv):
            chunks.append(plain.pop(0))
    return chunks


def _in_proj_body(h_ref, nw_ref, w_ref, cw_ref, cb_ref, *refs, tiles_per_seq):
    out_refs = dict(zip([n for n, _ in STREAMS], refs[:len(STREAMS)]))
    hn_ref = refs[len(STREAMS)]
    pad_refs = refs[len(STREAMS) + 1:]
    tm = h_ref.shape[0]

    @pl.when(pl.program_id(0) % tiles_per_seq == 0)
    def _():
        for pad_ref in pad_refs:
            pad_ref[0:SUBLANES, :] = jnp.zeros((SUBLANES, PROJ_COLS), F32)

    h = h_ref[...]
    ms = jnp.mean(h * h, axis=-1, keepdims=True)
    hn_ref[...] = (h * lax.rsqrt(ms + EPS) * nw_ref[...]).astype(BF16)

    def project(chunk):
        name, c0, w_off, width = chunk
        r = _dot(hn_ref[...], w_ref[:, w_off:w_off + width])
        if name in CONV_NAMES:
            pad_refs[(w_off - SSM_WIDTH) // PROJ_COLS][SUBLANES:SUBLANES + tm, :] = r
        else:
            out_refs[name][:, c0:c0 + width] = r.astype(out_refs[name].dtype)

    def finish(chunk):
        name, c0, w_off, width = chunk
        if name not in CONV_NAMES:
            return
        out = out_refs[name]
        cols = slice(w_off - SSM_WIDTH, w_off - SSM_WIDTH + width)
        pad_ref = pad_refs[(w_off - SSM_WIDTH) // PROJ_COLS]
        for r0 in range(0, tm, CONV_ROWS):
            blk = pad_ref[r0:r0 + CONV_ROWS + SUBLANES, :]
            acc = blk[SUBLANES:, :] * cw_ref[CONV_WIDTH - 1:CONV_WIDTH, cols]
            for j in range(CONV_WIDTH - 1):
                shifted = pltpu.roll(blk, CONV_WIDTH - 1 - j, 0)[SUBLANES:, :]
                acc = acc + shifted * cw_ref[j:j + 1, cols]
            if name in ("x", "b", "c"):
                acc = acc + cb_ref[:, cols]
            a = _silu(acc)
            if name in ("q", "k"):
                heads = []
                for hd in range(width // GDN_HEAD_DIM):
                    ah = a[:, hd * GDN_HEAD_DIM:(hd + 1) * GDN_HEAD_DIM]
                    ah = ah * lax.rsqrt(jnp.sum(ah * ah, axis=-1, keepdims=True) + EPS)
                    heads.append(ah * (GDN_HEAD_DIM ** -0.5) if name == "q" else ah)
                a = jnp.concatenate(heads, axis=1)
            out[r0:r0 + CONV_ROWS, c0:c0 + width] = a.astype(out.dtype)
        pad_ref[0:SUBLANES, :] = pad_ref[tm:tm + SUBLANES, :]

    chunks = _in_proj_chunks()
    project(chunks[0])
    for idx, chunk in enumerate(chunks):
        if idx + 1 < len(chunks):
            project(chunks[idx + 1])
        finish(chunk)


def _in_proj(h2d, norm_w, w_r, conv_w, conv_b, seqlen):
    n = h2d.shape[0]
    tm = min(TM_PROJ, seqlen)
    const = dict(pipeline_mode=pl.Buffered(1))
    out_shape = tuple(jax.ShapeDtypeStruct((n, w), F32 if name == "small" else BF16) for name, w in STREAMS)
    out_specs = tuple(pl.BlockSpec((tm, w), lambda i: (i, 0)) for _, w in STREAMS)
    return pl.pallas_call(
        functools.partial(_in_proj_body, tiles_per_seq=seqlen // tm),
        grid=(n // tm,),
        in_specs=[
            pl.BlockSpec((tm, D_MODEL), lambda i: (i, 0)),
            pl.BlockSpec((1, D_MODEL), lambda i: (0, 0)),
            pl.BlockSpec((D_MODEL, IN_PROJ_PAD), lambda i: (0, 0), **const),
            pl.BlockSpec((CONV_WIDTH, CONV_TOTAL), lambda i: (0, 0)),
            pl.BlockSpec((1, SSM_CONV_TOTAL), lambda i: (0, 0)),
        ],
        out_specs=out_specs,
        out_shape=out_shape,
        scratch_shapes=[pltpu.VMEM((tm, D_MODEL), BF16)]
        + [pltpu.VMEM((tm + SUBLANES, PROJ_COLS), F32) for _ in range(CONV_TOTAL // PROJ_COLS)],
        compiler_params=pltpu.CompilerParams(dimension_semantics=("arbitrary",),
                                             vmem_limit_bytes=VMEM_LIMIT_BYTES),
        name="in_proj",
    )(h2d, norm_w, w_r, conv_w, conv_b)


def _out_mlp_body(ymix_ref, x_ref, z_ref, gate_ref, h_ref, drow_ref, snorm_ref, gnorm_ref,
                  wout_ref, nw_ref, wup_ref, wdown_ref, fnw_ref, o_ref, *, final):
    h1 = h_ref[...]
    for g in range(SSM_GROUPS):
        cols = slice(g * SSM_GROUP_WIDTH, (g + 1) * SSM_GROUP_WIDTH)
        y = ymix_ref[:, cols].astype(F32) + drow_ref[:, cols] * x_ref[:, cols].astype(F32)
        y = y * _silu(z_ref[:, cols].astype(F32))
        ms = jnp.mean(y * y, axis=-1, keepdims=True)
        part = (y * lax.rsqrt(ms + EPS) * snorm_ref[:, cols]).astype(BF16)
        h1 = h1 + _dot(part, wout_ref[cols, :])
    for p in range(GDN_HEADS // 2):
        halves = []
        for hd in (2 * p, 2 * p + 1):
            hc = slice(hd * GDN_HEAD_DIM, (hd + 1) * GDN_HEAD_DIM)
            o = ymix_ref[:, SSM_WIDTH + hd * GDN_HEAD_DIM:SSM_WIDTH + (hd + 1) * GDN_HEAD_DIM].astype(F32)
            ms = jnp.mean(o * o, axis=-1, keepdims=True)
            o = o * lax.rsqrt(ms + EPS) * gnorm_ref[:, hc] * _silu(gate_ref[:, hc].astype(F32))
            halves.append(o.astype(BF16))
        rows = slice(SSM_WIDTH + 2 * p * GDN_HEAD_DIM, SSM_WIDTH + (2 * p + 2) * GDN_HEAD_DIM)
        h1 = h1 + _dot(jnp.concatenate(halves, axis=1), wout_ref[rows, :])
    ms = jnp.mean(h1 * h1, axis=-1, keepdims=True)
    hn = (h1 * lax.rsqrt(ms + EPS) * nw_ref[...]).astype(BF16)
    mlp = None
    for j in range(D_FF // D_MODEL):
        cols = slice(j * D_MODEL, (j + 1) * D_MODEL)
        up = _dot(hn, wup_ref[:, cols])
        act = jnp.square(jnp.maximum(up, 0.0)).astype(BF16)
        down = _dot(act, wdown_ref[cols, :])
        mlp = down if mlp is None else mlp + down
    acc = h1 + mlp
    if final:
        ms2 = jnp.mean(acc * acc, axis=-1, keepdims=True)
        acc = acc * lax.rsqrt(ms2 + EPS) * fnw_ref[...]
    o_ref[...] = acc


def _out_mlp(ymix, x_act, z, gate, h2d, drow, snorm, gnorm, w_out, norm_w, w_up, w_down, final_w, final):
    n = h2d.shape[0]
    tm = min(TM_PROJ, n)
    const = dict(pipeline_mode=pl.Buffered(1))

    def tok(width):
        return pl.BlockSpec((tm, width), lambda i: (i, 0))

    def row(width):
        return pl.BlockSpec((1, width), lambda i: (0, 0))

    return pl.pallas_call(
        functools.partial(_out_mlp_body, final=final),
        grid=(n // tm,),
        in_specs=[
            tok(MIX_WIDTH), tok(SSM_WIDTH), tok(SSM_WIDTH), tok(GDN_WIDTH), tok(D_MODEL),
            row(SSM_WIDTH), row(SSM_WIDTH), row(GDN_WIDTH),
            pl.BlockSpec((MIX_WIDTH, D_MODEL), lambda i: (0, 0), **const),
            row(D_MODEL),
            pl.BlockSpec((D_MODEL, D_FF), lambda i: (0, 0), **const),
            pl.BlockSpec((D_FF, D_MODEL), lambda i: (0, 0), **const),
            row(D_MODEL),
        ],
        out_specs=tok(D_MODEL),
        out_shape=jax.ShapeDtypeStruct((n, D_MODEL), F32),
        compiler_params=pltpu.CompilerParams(dimension_semantics=("arbitrary",),
                                             vmem_limit_bytes=VMEM_LIMIT_BYTES),
        name="out_mlp",
    )(ymix, x_act, z, gate, h2d, drow, snorm, gnorm, w_out, norm_w, w_up, w_down, final_w)


EXP_DT = 0
EXP_W = EXP_DT + SSM_WIDTH
EXP_EA = EXP_W + SSM_WIDTH
EXP_TOTAL = EXP_EA + SSM_WIDTH


def _mixer_body(x_ref, b_ref, c_ref, q_ref, k_ref, v_ref, small_ref, prow_ref, tri_ref, e_ssd_ref,
                out_ref,
                exp_ref, xdt_ref, xw_ref, ht_ref, s_ref, *, tile):
    nchunk = tile // CHUNK

    @pl.when(pl.program_id(1) == 0)
    def _():
        ht_ref[...] = jnp.zeros_like(ht_ref)
        s_ref[...] = jnp.zeros_like(s_ref)

    lane = _lane_iota((tile, LANES))
    sm = small_ref[...]
    bias = prow_ref[0:1, :]
    neg_a = -jnp.exp(prow_ref[1:2, :])
    sp = _softplus(sm + bias)
    is_dt = lane < COL_BETA
    is_beta = (lane >= COL_BETA) & (lane < COL_G)
    is_g = (lane >= COL_G) & (lane < SPLIT_LANES)
    steps = jnp.where(is_dt | is_g, sp * neg_a, 0.0)
    sc = _dot(tri_ref[...], _compress3(steps))
    cs = sc + pltpu.roll(sc, LANES - SPLIT_LANES, 1) + pltpu.roll(sc, LANES - 2 * SPLIT_LANES, 1)
    cs = jnp.where(lane < SPLIT_LANES, cs, 0.0)
    beta = jnp.where(is_beta, 1.0 / (1.0 + jnp.exp(-sm)), 0.0)
    dt = jnp.where(is_dt, sp, 0.0)
    to_end = jnp.concatenate(
        [jnp.exp(cs[(c + 1) * CHUNK - 1:(c + 1) * CHUNK, :] - cs[c * CHUNK:(c + 1) * CHUNK, :]) for c in range(nchunk)],
        axis=0)
    wfac = jnp.where(is_dt, dt * to_end, 0.0)
    ecs = jnp.exp(cs)
    ea = jnp.where(is_dt, ecs, 0.0)
    exp_ref[:, EXP_DT:EXP_DT + SSM_WIDTH] = _dot(_compress3(dt), e_ssd_ref[...])
    exp_ref[:, EXP_W:EXP_W + SSM_WIDTH] = _dot(_compress3(wfac), e_ssd_ref[...])
    exp_ref[:, EXP_EA:EXP_EA + SSM_WIDTH] = _dot(_compress3(ea), e_ssd_ref[...])

    xa = x_ref[...].astype(F32)
    xdt = xa * exp_ref[:, EXP_DT:EXP_DT + SSM_WIDTH]
    even_head = (_lane_iota((tile, SSM_WIDTH)) // SSM_HEAD_DIM) % 2 == 0
    xdt_ref[0] = jnp.where(even_head, xdt, 0.0).astype(BF16)
    xdt_ref[1] = jnp.where(even_head, 0.0, xdt).astype(BF16)
    xw_ref[...] = (xa * exp_ref[:, EXP_W:EXP_W + SSM_WIDTH]).astype(BF16)

    row_i = lax.broadcasted_iota(jnp.int32, (CHUNK, CHUNK), 0)
    col_i = lax.broadcasted_iota(jnp.int32, (CHUNK, CHUNK), 1)
    incl = row_i >= col_i
    strict = row_i > col_i
    eye = (row_i == col_i).astype(F32)
    merge_masks = [(row_i // 2 == col_i // 2) & strict]
    size = 2
    while size < CHUNK:
        merge_masks.append((row_i // (2 * size) == col_i // (2 * size))
                           & ((row_i // size) % 2 == 1) & ((col_i // size) % 2 == 0))
        size *= 2

    chunk_rows = [slice(c * CHUNK, (c + 1) * CHUNK) for c in range(nchunk)]
    chunk_last = [slice((c + 1) * CHUNK - 1, (c + 1) * CHUNK) for c in range(nchunk)]
    cs_t = [cs[rows, :].T for rows in chunk_rows]


    for c in range(nchunk):
        rows, last = chunk_rows[c], chunk_last[c]
        bgs = [b_ref[rows, g * SSM_STATE:(g + 1) * SSM_STATE] for g in range(SSM_GROUPS)]
        cgs = [c_ref[rows, g * SSM_STATE:(g + 1) * SSM_STATE] for g in range(SSM_GROUPS)]
        cbs = [_dot_nt(cgs[g], bgs[g]) for g in range(SSM_GROUPS)]
        y_pairs = []
        for p in range(SSM_HEADS // 2):
            ms = []
            for hh in (2 * p, 2 * p + 1):
                a_col = jnp.broadcast_to(cs[rows, hh:hh + 1], (CHUNK, CHUNK))
                a_row = cs_t[c][hh:hh + 1, :]
                decay = jnp.exp(jnp.where(incl, a_col - a_row, NEG_BIG))
                ms.append((cbs[hh // (SSM_HEADS // SSM_GROUPS)] * decay).astype(BF16))
            xs = jnp.concatenate([xdt_ref[i, rows, p * LANES:(p + 1) * LANES] for i in range(2)], axis=0)
            y_pairs.append(_dot(jnp.concatenate(ms, axis=1), xs))
        for g in range(SSM_GROUPS):
            gcols = slice(g * SSM_GROUP_WIDTH, (g + 1) * SSM_GROUP_WIDTH)
            ea_g = exp_ref[rows, EXP_EA + g * SSM_GROUP_WIDTH:EXP_EA + (g + 1) * SSM_GROUP_WIDTH]
            ea_last = exp_ref[last, EXP_EA + g * SSM_GROUP_WIDTH:EXP_EA + (g + 1) * SSM_GROUP_WIDTH]
            h_prev = ht_ref[g]
            y_off = _dot(cgs[g], h_prev.astype(BF16)) * ea_g
            out_ref[rows, gcols] = (jnp.concatenate(y_pairs[2 * g:2 * g + 2], axis=1) + y_off).astype(out_ref.dtype)
            ht_ref[g] = h_prev * ea_last + _dot_tn(bgs[g], xw_ref[rows, gcols])

    def head_col(tile_val, rows, lane_idx):
        return jnp.broadcast_to(tile_val[rows, lane_idx:lane_idx + 1], (CHUNK, GDN_HEAD_DIM))

    units = [(c, hd) for c in range(nchunk) for hd in range(GDN_HEADS)]
    a_mats, neg_as, qks, rhss, qdecs, kdecs, s_decays = {}, {}, {}, {}, {}, {}, {}
    for u in units:
        c, hd = u
        rows, last = chunk_rows[c], chunk_last[c]
        hcols = slice(hd * GDN_HEAD_DIM, (hd + 1) * GDN_HEAD_DIM)
        kh_b = k_ref[rows, hcols]
        qh = q_ref[rows, hcols].astype(F32)
        kh = kh_b.astype(F32)
        vh = v_ref[rows, hcols].astype(F32)
        be = head_col(beta, rows, COL_BETA + hd)
        gc = head_col(cs, rows, COL_G + hd)
        eg = head_col(ecs, rows, COL_G + hd)
        ek = head_col(to_end, rows, COL_G + hd)
        g_row = cs_t[c][COL_G + hd:COL_G + hd + 1, :]
        kb = kh * be
        gram = _dot_nt(jnp.concatenate([kb, qh], axis=0).astype(BF16), kh_b)
        decay = jnp.exp(jnp.where(incl, gc - g_row, NEG_BIG))
        a_mats[u] = jnp.where(strict, gram[0:CHUNK, :] * decay, 0.0)
        neg_as[u] = (-a_mats[u]).astype(BF16)
        qks[u] = (gram[CHUNK:2 * CHUNK, :] * decay).astype(BF16)
        rhss[u] = jnp.concatenate([vh * be, kb * eg], axis=1).astype(BF16)
        qdecs[u] = (qh * eg).astype(BF16)
        kdecs[u] = (kh * ek).astype(BF16)
        s_decays[u] = jnp.broadcast_to(ecs[last, COL_G + hd:COL_G + hd + 1], (1, GDN_HEAD_DIM))

    tinv = {u: eye - jnp.where(merge_masks[0], a_mats[u], 0.0) for u in units}
    for m in merge_masks[1:]:
        tinv_b = {u: tinv[u].astype(BF16) for u in units}
        inner = {u: _dot(neg_as[u], tinv_b[u]).astype(BF16) for u in units}
        tinv = {u: jnp.where(m, _dot(tinv_b[u], inner[u]), tinv[u]) for u in units}
    uws = {u: _dot(tinv[u].astype(BF16), rhss[u]) for u in units}

    states = [s_ref[hd] for hd in range(GDN_HEADS)]
    for c in range(nchunk):
        rows = chunk_rows[c]
        wqs = [_dot(jnp.concatenate([uws[(c, hd)][:, GDN_HEAD_DIM:].astype(BF16), qdecs[(c, hd)]], axis=0),
                    states[hd].astype(BF16)) for hd in range(GDN_HEADS)]
        v_news = [(uws[(c, hd)][:, 0:GDN_HEAD_DIM] - wqs[hd][0:CHUNK, :]).astype(BF16) for hd in range(GDN_HEADS)]
        for hd in range(GDN_HEADS):
            out_ref[rows, SSM_WIDTH + hd * GDN_HEAD_DIM:SSM_WIDTH + (hd + 1) * GDN_HEAD_DIM] = (
                wqs[hd][CHUNK:2 * CHUNK, :] + _dot(qks[(c, hd)], v_news[hd])).astype(out_ref.dtype)
        states = [states[hd] * s_decays[(c, hd)] + _dot_tn(kdecs[(c, hd)], v_news[hd]) for hd in range(GDN_HEADS)]
    for hd in range(GDN_HEADS):
        s_ref[hd] = states[hd]


def _expansion_matrix(first_row, heads, lanes_per_head):
    r = jnp.arange(LANES)[:, None]
    col_head = jnp.arange(heads * lanes_per_head)[None, :] // lanes_per_head
    src = (r % SPLIT_LANES) - first_row
    valid = r < 3 * SPLIT_LANES
    return ((src == col_head) & valid).astype(BF16)


def _mixer(streams, bsz, seqlen, prow):
    tile = min(T_MIX, seqlen)
    nt = seqlen // tile
    r = jnp.arange(tile)
    tri = ((r[:, None] >= r[None, :]) & (r[:, None] // CHUNK == r[None, :] // CHUNK)).astype(BF16)
    e_ssd = _expansion_matrix(COL_DT, SSM_HEADS, SSM_HEAD_DIM)

    def tok_spec(width):
        return pl.BlockSpec((None, tile, width), lambda b, t: (b, t, 0))

    def const_spec(arr):
        return pl.BlockSpec(arr.shape, lambda b, t: (0,) * arr.ndim)

    names = ("x", "b", "c", "q", "k", "v", "small")
    widths = dict(STREAMS)
    stream_arrays = [streams[name].reshape(bsz, seqlen, widths[name]) for name in names]
    consts = [prow, tri, e_ssd]
    return pl.pallas_call(
        functools.partial(_mixer_body, tile=tile),
        grid=(bsz, nt),
        in_specs=[tok_spec(widths[name]) for name in names] + [const_spec(a) for a in consts],
        out_specs=tok_spec(MIX_WIDTH),
        out_shape=jax.ShapeDtypeStruct((bsz, seqlen, MIX_WIDTH), BF16),
        scratch_shapes=[
            pltpu.VMEM((tile, EXP_TOTAL), F32),
            pltpu.VMEM((2, tile, SSM_WIDTH), BF16),
            pltpu.VMEM((tile, SSM_WIDTH), BF16),
            pltpu.VMEM((SSM_GROUPS, SSM_STATE, SSM_GROUP_WIDTH), F32),
            pltpu.VMEM((GDN_HEADS, GDN_HEAD_DIM, GDN_HEAD_DIM), F32),
        ],
        compiler_params=pltpu.CompilerParams(dimension_semantics=("arbitrary", "arbitrary"),
                                             vmem_limit_bytes=VMEM_LIMIT_BYTES),
        name="mixer",
    )(*stream_arrays, *consts)


def _rearranged_in_proj_weight(w_in):
    small = jnp.concatenate([w_in[:, OFF_DT:OFF_DT + SSM_HEADS], w_in[:, OFF_BETA:OFF_BETA + GDN_HEADS],
                             w_in[:, OFF_A:OFF_A + GDN_HEADS],
                             jnp.zeros((D_MODEL, LANES - SPLIT_LANES), w_in.dtype)], axis=1)
    parts = [w_in[:, OFF_Z:OFF_Z + SSM_WIDTH], w_in[:, OFF_X:OFF_X + SSM_WIDTH],
             w_in[:, OFF_B:OFF_B + SSM_BC_WIDTH], w_in[:, OFF_C:OFF_C + SSM_BC_WIDTH],
             w_in[:, OFF_Q:OFF_Q + GDN_WIDTH], w_in[:, OFF_K:OFF_K + GDN_WIDTH], w_in[:, OFF_V:OFF_V + GDN_WIDTH],
             w_in[:, OFF_GATE:OFF_GATE + GDN_WIDTH], small]
    return jnp.concatenate(parts, axis=1).astype(BF16)


def _scalar_rows(ssm_dt_bias, ssm_a_log, gdn_dt_bias, gdn_a_log):
    zeros8 = jnp.zeros((GDN_HEADS,), F32)
    tail = jnp.zeros((LANES - SPLIT_LANES,), F32)
    bias = jnp.concatenate([ssm_dt_bias.astype(F32), zeros8, gdn_dt_bias.astype(F32), tail])
    alog = jnp.concatenate([ssm_a_log.astype(F32), zeros8, gdn_a_log.astype(F32), tail])
    return jnp.concatenate([bias[None], alog[None], jnp.zeros((SUBLANES - 2, LANES), F32)], axis=0)


def kernel(x, attn_norm_w, w_in, ssm_conv_w, ssm_conv_b, ssm_dt_bias, ssm_a_log, ssm_d, ssm_norm_w, gdn_conv_w,
           gdn_dt_bias, gdn_a_log, gdn_norm_w, w_out, mlp_norm_w, w_up, w_down, final_norm_w):
    bsz, seqlen, _ = x.shape
    depth = w_in.shape[0]
    assert seqlen % CHUNK == 0 and seqlen % min(TM_PROJ, seqlen) == 0
    h = x.reshape(bsz * seqlen, D_MODEL).astype(F32)
    final_w = final_norm_w.reshape(1, D_MODEL).astype(F32)
    for layer in range(depth):
        conv_w = jnp.concatenate([ssm_conv_w[layer], gdn_conv_w[layer]], axis=1).astype(F32)
        outs = _in_proj(h, attn_norm_w[layer].reshape(1, D_MODEL).astype(F32), _rearranged_in_proj_weight(w_in[layer]),
                        conv_w, ssm_conv_b[layer].reshape(1, SSM_CONV_TOTAL).astype(F32), seqlen)
        streams = {name: o for (name, _), o in zip(STREAMS, outs)}
        ymix = _mixer(streams, bsz, seqlen,
                      _scalar_rows(ssm_dt_bias[layer], ssm_a_log[layer], gdn_dt_bias[layer], gdn_a_log[layer]))
        h = _out_mlp(ymix.reshape(bsz * seqlen, MIX_WIDTH), streams["x"], streams["z"], streams["gate"], h,
                     jnp.repeat(ssm_d[layer].astype(F32), SSM_HEAD_DIM).reshape(1, SSM_WIDTH),
                     ssm_norm_w[layer].reshape(1, SSM_WIDTH).astype(F32),
                     jnp.tile(gdn_norm_w[layer].astype(F32), GDN_HEADS).reshape(1, GDN_WIDTH),
                     w_out[layer].astype(BF16), mlp_norm_w[layer].reshape(1, D_MODEL).astype(F32),
                     w_up[layer].astype(BF16), w_down[layer].astype(BF16), final_w, final=(layer == depth - 1))
    return h.reshape(bsz, seqlen, D_MODEL).astype(x.dtype)
```
